```python
import math
import jax
import jax.numpy as jnp
from jax import lax
import numpy as np

D_MODEL = 1024
BATCH = 16
SEQ = 256
DEPTH = 2
DEC_BATCH = 8
DEC_SEQ = 2048
PAST_LEN = 256

GRID_W = 64
EPS = 1e-6
N_MOD = 6

GLA_HEADS = 4
GLA_DK = 64
GLA_DV = 128
GLA_RANK = 16
GLA_TAU = 16.0
GLA_CHUNK = 64

SWA_HEADS = 8
SWA_KV_HEADS = 2
SWA_GROUP = SWA_HEADS // SWA_KV_HEADS
SWA_HD = 64
SWA_WINDOW = 128
SWA_BLOCK = 128
SWA_SCALE = SWA_HD ** -0.5
ROPE_BASE = 10000.0

DN_HEADS = 8
DN_DK = 128
DN_DV = 128
DN_CONV = 5
DN_CHUNK = 64

D_FF_DENSE = 2816
N_EXPERTS = 8
TOP_K = 2
D_FF_EXPERT = 3584

L0_SIZES = (GLA_HEADS * GLA_DK, GLA_HEADS * GLA_DK, GLA_HEADS * GLA_DV, GLA_HEADS * GLA_DV,
            GLA_RANK, GLA_RANK,
            SWA_HEADS * SWA_HD, SWA_KV_HEADS * SWA_HD, SWA_KV_HEADS * SWA_HD)
L0_IN = sum(L0_SIZES)
L0_OUT = GLA_HEADS * GLA_DV + SWA_HEADS * SWA_HD
DN_QKV_SIZES = (DN_HEADS * DN_DK, DN_HEADS * DN_DK, DN_HEADS * DN_DV)
DN_QKV = sum(DN_QKV_SIZES)
L1_SIZES = (DN_QKV, DN_HEADS * DN_DV, DN_HEADS, DN_HEADS, DN_HEADS, DN_HEADS)
L1_IN = sum(L1_SIZES)
L1_OUT = DN_HEADS * DN_DV

kernel_name = "hybrid_gla_swa_deltanet_diffusion_step"


def _split_cols(x, sizes):
    cuts, acc = [], 0
    for s in sizes[:-1]:
        acc += s
        cuts.append(acc)
    return jnp.split(x, cuts, axis=-1)


def _rmsnorm(x, g):
    xf = x.astype(jnp.float32)
    y = xf * lax.rsqrt(jnp.mean(xf * xf, axis=-1, keepdims=True) + EPS)
    return (y * g.astype(jnp.float32)).astype(x.dtype)


def _l2norm(x):
    xf = x.astype(jnp.float32)
    return xf * lax.rsqrt(jnp.sum(xf * xf, axis=-1, keepdims=True) + EPS)


def _modulation(cond, w, b):
    m = jnp.expand_dims(jax.nn.silu(cond) @ w + b, -2)
    return jnp.split(m, N_MOD, axis=-1)


def _swiglu(h, w1, w3, w2):
    return (jax.nn.silu(h @ w1) * (h @ w3)) @ w2


def _moe_swiglu(h, router, w1, w3, w2):
    logits = (h @ router).astype(jnp.float32)
    top_v, top_i = lax.top_k(logits, TOP_K)
    gates = jax.nn.softmax(top_v, axis=-1)
    combine = jnp.sum(jax.nn.one_hot(top_i, N_EXPERTS, dtype=jnp.float32) * gates[..., None], axis=-2)
    out = jnp.zeros_like(h)
    for e in range(N_EXPERTS):
        out = out + combine[..., e:e + 1].astype(h.dtype) * _swiglu(h, w1[e], w3[e], w2[e])
    return out


def _axial_rope_angles(n_tokens):
    rows = n_tokens // GRID_W
    row = jnp.repeat(jnp.arange(rows), GRID_W).astype(jnp.float32)
    col = (jnp.arange(rows * GRID_W) % GRID_W).astype(jnp.float32)
    half = SWA_HD // 2
    inv_freq = ROPE_BASE ** (-jnp.arange(0, half, 2, dtype=jnp.float32) / half)
    return row[:, None] * inv_freq, col[:, None] * inv_freq


def _rotate(x, ang):
    m = ang.shape[-1]
    cos = jnp.cos(ang)[:, None, :].astype(x.dtype)
    sin = jnp.sin(ang)[:, None, :].astype(x.dtype)
    x1, x2 = x[..., :m], x[..., m:]
    return jnp.concatenate([x1 * cos - x2 * sin, x1 * sin + x2 * cos], axis=-1)


def _axial_rope(x, ang_row, ang_col):
    half = x.shape[-1] // 2
    return jnp.concatenate([_rotate(x[..., :half], ang_row), _rotate(x[..., half:], ang_col)], axis=-1)


def _sink_softmax(s, sink):
    sk = jnp.broadcast_to(sink.astype(jnp.float32).reshape(1, SWA_KV_HEADS, SWA_GROUP, 1, 1),
                          s.shape[:-1] + (1,))
    p = jax.nn.softmax(jnp.concatenate([s, sk], axis=-1), axis=-1)
    return p[..., :-1]


def _context_attention(q, k, v, sink):
    B, L = q.shape[:2]
    nb = L // SWA_BLOCK
    qb = q.reshape(B, nb, SWA_BLOCK, SWA_KV_HEADS, SWA_GROUP, SWA_HD).swapaxes(0, 1)

    def block(qi):
        s = jnp.einsum('bqkgd,bskd->bkgqs', qi, k).astype(jnp.float32) * SWA_SCALE
        p = _sink_softmax(s, sink).astype(v.dtype)
        return jnp.einsum('bkgqs,bskd->bqkgd', p, v)

    o = lax.map(block, qb)
    return o.swapaxes(0, 1).reshape(B, L, SWA_HEADS * SWA_HD)


def _latent_attention(q, k, v, k_ctx, v_ctx, sink):
    B, L = q.shape[:2]
    nb = L // SWA_BLOCK
    n_ctx = k_ctx.shape[1]
    pad = ((0, 0), (SWA_BLOCK, SWA_BLOCK), (0, 0), (0, 0))
    kp = jnp.pad(k, pad)
    vp = jnp.pad(v, pad)
    qb = q.reshape(B, nb, SWA_BLOCK, SWA_KV_HEADS, SWA_GROUP, SWA_HD).swapaxes(0, 1)
    qpos = jnp.arange(SWA_BLOCK)[:, None]
    kpos = jnp.arange(3 * SWA_BLOCK)[None, :]

    def block(args):
        qi, i = args
        kw = lax.dynamic_slice_in_dim(kp, i * SWA_BLOCK, 3 * SWA_BLOCK, axis=1)
        vw = lax.dynamic_slice_in_dim(vp, i * SWA_BLOCK, 3 * SWA_BLOCK, axis=1)
        t = i * SWA_BLOCK + qpos
        j = i * SWA_BLOCK - SWA_BLOCK + kpos
        mask = (jnp.abs(t - j) <= SWA_WINDOW) & (j >= 0) & (j < L)
        s_loc = jnp.einsum('bqkgd,bskd->bkgqs', qi, kw).astype(jnp.float32) * SWA_SCALE
        s_loc = jnp.where(mask, s_loc, -jnp.inf)
        s_ctx = jnp.einsum('bqkgd,bskd->bkgqs', qi, k_ctx).astype(jnp.float32) * SWA_SCALE
        p = _sink_softmax(jnp.concatenate([s_ctx, s_loc], axis=-1), sink).astype(v.dtype)
        return (jnp.einsum('bkgqs,bskd->bqkgd', p[..., :n_ctx], v_ctx)
                + jnp.einsum('bkgqs,bskd->bqkgd', p[..., n_ctx:], vw))

    o = lax.map(block, (qb, jnp.arange(nb)))
    return o.swapaxes(0, 1).reshape(B, L, SWA_HEADS * SWA_HD)


def _chunks(x, c):
    B, L, H, d = x.shape
    return x.reshape(B, L // c, c, H, d).transpose(1, 0, 3, 2, 4)


def _unchunks(x):
    n, B, H, c, d = x.shape
    return x.transpose(1, 0, 3, 2, 4).reshape(B, n * c, H, d)


def _gla_scan(q, k, v, log_a, s0):
    f32 = jnp.float32
    qc, kc, vc, gc = (_chunks(t.astype(f32), GLA_CHUNK) for t in (q, k, v, log_a))
    b = jnp.cumsum(gc, axis=-2)
    b_last = b[..., -1:, :]
    q_dec = qc * jnp.exp(b)
    k_inv = kc * jnp.exp(-b)
    k_end = kc * jnp.exp(b_last - b)
    lower = jnp.tril(jnp.ones((GLA_CHUNK, GLA_CHUNK), dtype=bool))
    attn = jnp.where(lower, jnp.einsum('nbhtd,nbhsd->nbhts', q_dec, k_inv), 0.0)
    o_intra = jnp.einsum('nbhts,nbhsv->nbhtv', attn, vc)

    def step(S, xs):
        q_i, k_i, v_i, d_i = xs
        o_i = jnp.einsum('bhtd,bhdv->bhtv', q_i, S)
        S = S * d_i[..., None] + jnp.einsum('bhsd,bhsv->bhdv', k_i, v_i)
        return S, o_i

    s_fin, o_inter = lax.scan(step, s0.astype(f32), (q_dec, k_end, vc, jnp.exp(b_last[..., 0, :])))
    return _unchunks(o_intra + o_inter), s_fin


def _delta_scan(q, k, v, g, beta, s0):
    f32 = jnp.float32
    B, L, H, _ = q.shape
    C = DN_CHUNK
    n = L // C
    qc, kc, vc = (_chunks(t.astype(f32), C) for t in (q, k, v))
    gc = g.astype(f32).reshape(B, n, C, H).transpose(1, 0, 3, 2)
    bc = beta.astype(f32).reshape(B, n, C, H).transpose(1, 0, 3, 2)
    G = jnp.cumsum(gc, axis=-1)
    lower = jnp.tril(jnp.ones((C, C), dtype=bool))
    strict = jnp.tril(jnp.ones((C, C), dtype=bool), k=-1)
    decay = jnp.exp(jnp.where(lower, G[..., :, None] - G[..., None, :], -jnp.inf))
    k_beta = kc * bc[..., None]
    a_mat = jnp.where(strict, jnp.einsum('nbhid,nbhjd->nbhij', k_beta, kc) * decay, 0.0)
    eye = jnp.eye(C, dtype=f32)
    rhs = jnp.concatenate([vc * bc[..., None], k_beta * jnp.exp(G)[..., None]], axis=-1)
    sol = lax.linalg.triangular_solve(eye + a_mat, rhs, left_side=True, lower=True, unit_diagonal=True)
    u, w = sol[..., :DN_DV], sol[..., DN_DV:]
    qk = jnp.einsum('nbhid,nbhjd->nbhij', qc, kc) * decay
    q_dec = qc * jnp.exp(G)[..., None]
    k_end = kc * jnp.exp(G[..., -1:] - G)[..., None]
    d_last = jnp.exp(G[..., -1])

    def step(S, xs):
        u_i, w_i, q_i, qk_i, k_i, d_i = xs
        v_new = u_i - w_i @ S
        o_i = q_i @ S + qk_i @ v_new
        S = S * d_i[..., None, None] + jnp.einsum('bhsd,bhsv->bhdv', k_i, v_new)
        return S, o_i

    s_fin, o = lax.scan(step, s0.astype(f32), (u, w, q_dec, qk, k_end, d_last))
    return _unchunks(o), s_fin


def _centred_dwconv(x, w):
    width, ch = w.shape
    return lax.conv_general_dilated(x, w[:, None, :].astype(x.dtype), window_strides=(1,),
                                    padding=[(width // 2, width // 2)],
                                    dimension_numbers=('NWC', 'WIO', 'NWC'),
                                    feature_group_count=ch)


def _mixer_ab(h, p, ctx):
    B, L, _ = h.shape
    f32 = jnp.float32
    gq, gk, gv, gr, glr_f, glr_b, sq, sk, sv = _split_cols(h @ p['in_w'], L0_SIZES)
    q = gq.reshape(B, L, GLA_HEADS, GLA_DK) * (GLA_DK ** -0.5)
    k = gk.reshape(B, L, GLA_HEADS, GLA_DK)
    v = gv.reshape(B, L, GLA_HEADS, GLA_DV)

    def log_gate(z, d):
        pre = (z @ p['gla_w2'][d] + p['gla_b'][d]).astype(f32)
        return (jax.nn.log_sigmoid(pre) / GLA_TAU).reshape(B, L, GLA_HEADS, GLA_DK)

    la_f, la_b = log_gate(glr_f, 0), log_gate(glr_b, 1)
    s0 = jnp.zeros((B, 2, GLA_HEADS, GLA_DK, GLA_DV), f32) if ctx is None else ctx[0]
    o_f, s_f = _gla_scan(q, k, v, la_f, s0[:, 0])
    o_b, s_b = _gla_scan(jnp.flip(q, 1), jnp.flip(k, 1), jnp.flip(v, 1), jnp.flip(la_b, 1), s0[:, 1])
    o = o_f + jnp.flip(o_b, 1)
    o = _rmsnorm(o, p['gla_onorm']) * jax.nn.silu(gr.reshape(B, L, GLA_HEADS, GLA_DV).astype(f32))
    o_gla = o.reshape(B, L, GLA_HEADS * GLA_DV).astype(h.dtype)

    sq = sq.reshape(B, L, SWA_HEADS, SWA_HD)
    sk = sk.reshape(B, L, SWA_KV_HEADS, SWA_HD)
    sv = sv.reshape(B, L, SWA_KV_HEADS, SWA_HD)
    if ctx is None:
        qg = sq.reshape(B, L, SWA_KV_HEADS, SWA_GROUP, SWA_HD)
        o_swa = _context_attention(qg, sk, sv, p['sink'])
        new_ctx = (jnp.stack([s_f, s_b], axis=1).astype(h.dtype), sk, sv)
    else:
        ang_r, ang_c = _axial_rope_angles(L)
        qg = _axial_rope(sq, ang_r, ang_c).reshape(B, L, SWA_KV_HEADS, SWA_GROUP, SWA_HD)
        kr = _axial_rope(sk, ang_r, ang_c)
        o_swa = _latent_attention(qg, kr, sv, ctx[1], ctx[2], p['sink'])
        new_ctx = None
    y = jnp.concatenate([o_gla, o_swa.astype(h.dtype)], axis=-1) @ p['out_w']
    return y, new_ctx


def _mixer_c(h, p, ctx):
    B, L, _ = h.shape
    f32 = jnp.float32
    qkv, z, a_f, a_b, b_f, b_b = _split_cols(h @ p['in_w'], L1_SIZES)
    qkv = jax.nn.silu(_centred_dwconv(qkv, p['conv_w']))
    q, k, v = _split_cols(qkv, DN_QKV_SIZES)
    q = _l2norm(q.reshape(B, L, DN_HEADS, DN_DK)) * (DN_DK ** -0.5)
    k = _l2norm(k.reshape(B, L, DN_HEADS, DN_DK))
    v = v.reshape(B, L, DN_HEADS, DN_DV)
    a_log = p['a_log'].astype(f32)
    dt_bias = p['dt_bias'].astype(f32)
    g_f = -jnp.exp(a_log[0]) * jax.nn.softplus(a_f.astype(f32) + dt_bias[0])
    g_b = -jnp.exp(a_log[1]) * jax.nn.softplus(a_b.astype(f32) + dt_bias[1])
    beta_f = jax.nn.sigmoid(b_f.astype(f32))
    beta_b = jax.nn.sigmoid(b_b.astype(f32))
    s0 = jnp.zeros((B, 2, DN_HEADS, DN_DK, DN_DV), f32) if ctx is None else ctx[0]
    o_f, s_f = _delta_scan(q, k, v, g_f, beta_f, s0[:, 0])
    o_b, s_b = _delta_scan(jnp.flip(q, 1), jnp.flip(k, 1), jnp.flip(v, 1),
                           jnp.flip(g_b, 1), jnp.flip(beta_b, 1), s0[:, 1])
    o = o_f + jnp.flip(o_b, 1)
    o = _rmsnorm(o, p['dn_onorm']) * jax.nn.silu(z.reshape(B, L, DN_HEADS, DN_DV).astype(f32))
    y = o.reshape(B, L, L1_OUT).astype(h.dtype) @ p['out_w']
    new_ctx = (jnp.stack([s_f, s_b], axis=1).astype(h.dtype),) if ctx is None else None
    return y, new_ctx


def _layer(x, cond, l, p, ctx):
    shift1, scale1, gate1, shift2, scale2, gate2 = _modulation(cond, p['ada_w'], p['ada_b'])
    h = _rmsnorm(x, p['norm1']) * (1.0 + scale1) + shift1
    if l % 2 == 0:
        y, new_ctx = _mixer_ab(h, p, ctx)
    else:
        y, new_ctx = _mixer_c(h, p, ctx)
    x = x + gate1 * y
    h = _rmsnorm(x, p['norm2']) * (1.0 + scale2) + shift2
    if l % 2 == 0:
        y = _swiglu(h, p['ffn_w1'], p['ffn_w3'], p['ffn_w2'])
    else:
        y = _moe_swiglu(h, p['router'], p['moe_w1'], p['moe_w3'], p['moe_w2'])
    return x + gate2 * y, new_ctx


def setup_inputs(seed: int = 0) -> dict:
    key = jax.random.key(seed)
    keys = list(jax.random.split(key, 64))
    counter = [0]

    def nk():
        counter[0] += 1
        return keys[counter[0] - 1]

    def nrm(shape, scale=1.0):
        return jax.random.normal(nk(), shape, jnp.float32) * scale

    def gain(n):
        return 1.0 + nrm((n,), 0.01)

    D = D_MODEL
    inp = {}
    inp['x_prompt'] = nrm((BATCH, SEQ, D))
    inp['x_sample'] = nrm((DEC_BATCH, DEC_SEQ, D))
    inp['c'] = nrm((DEC_BATCH, D))
    inp['state_gla_l0'] = nrm((DEC_BATCH, 2, GLA_HEADS, GLA_DK, GLA_DV), 0.1)
    inp['cache_k_l0'] = nrm((DEC_BATCH, PAST_LEN, SWA_KV_HEADS, SWA_HD))
    inp['cache_v_l0'] = nrm((DEC_BATCH, PAST_LEN, SWA_KV_HEADS, SWA_HD))
    inp['state_dn_l1'] = nrm((DEC_BATCH, 2, DN_HEADS, DN_DK, DN_DV), 0.1)
    inp['c_ctx'] = nrm((D,))
    inp['final_norm'] = gain(D)
    inp['ada_w_l0'] = nrm((D, N_MOD * D), 0.5 * D ** -0.5)
    inp['ada_b_l0'] = nrm((N_MOD * D,), 0.01)
    inp['norm1_l0'] = gain(D)
    inp['norm2_l0'] = gain(D)
    inp['in_w_l0'] = nrm((D, L0_IN), D ** -0.5)
    inp['gla_w2_l0'] = nrm((2, GLA_RANK, GLA_HEADS * GLA_DK), GLA_RANK ** -0.5)
    inp['gla_b_l0'] = nrm((2, GLA_HEADS * GLA_DK), 0.1)
    inp['gla_onorm_l0'] = gain(GLA_DV)
    inp['sink_l0'] = nrm((SWA_HEADS,), 0.5)
    inp['out_w_l0'] = nrm((L0_OUT, D), L0_OUT ** -0.5)
    inp['ffn_w1_l0'] = nrm((D, D_FF_DENSE), D ** -0.5)
    inp['ffn_w3_l0'] = nrm((D, D_FF_DENSE), D ** -0.5)
    inp['ffn_w2_l0'] = nrm((D_FF_DENSE, D), D_FF_DENSE ** -0.5)
    inp['ada_w_l1'] = nrm((D, N_MOD * D), 0.5 * D ** -0.5)
    inp['ada_b_l1'] = nrm((N_MOD * D,), 0.01)
    inp['norm1_l1'] = gain(D)
    inp['norm2_l1'] = gain(D)
    inp['in_w_l1'] = nrm((D, L1_IN), D ** -0.5)
    inp['conv_w_l1'] = nrm((DN_CONV, DN_QKV), DN_CONV ** -0.5)
    inp['a_log_l1'] = jnp.log(jax.random.uniform(nk(), (2, DN_HEADS), jnp.float32, 1.0, 16.0))
    dt = jnp.exp(jax.random.uniform(nk(), (2, DN_HEADS), jnp.float32, math.log(1e-3), math.log(1e-1)))
    inp['dt_bias_l1'] = dt + jnp.log(-jnp.expm1(-dt))
    inp['dn_onorm_l1'] = gain(DN_DV)
    inp['out_w_l1'] = nrm((L1_OUT, D), L1_OUT ** -0.5)
    inp['router_l1'] = nrm((D, N_EXPERTS), D ** -0.5)
    inp['moe_w1_l1'] = nrm((N_EXPERTS, D, D_FF_EXPERT), D ** -0.5)
    inp['moe_w3_l1'] = nrm((N_EXPERTS, D, D_FF_EXPERT), D ** -0.5)
    inp['moe_w2_l1'] = nrm((N_EXPERTS, D_FF_EXPERT, D), D_FF_EXPERT ** -0.5)
    return inp


def reference(x_prompt, x_sample, c, state_gla_l0, cache_k_l0, cache_v_l0, state_dn_l1,
              c_ctx, final_norm,
              ada_w_l0, ada_b_l0, norm1_l0, norm2_l0, in_w_l0, gla_w2_l0, gla_b_l0, gla_onorm_l0,
              sink_l0, out_w_l0, ffn_w1_l0, ffn_w3_l0, ffn_w2_l0,
              ada_w_l1, ada_b_l1, norm1_l1, norm2_l1, in_w_l1, conv_w_l1, a_log_l1, dt_bias_l1,
              dn_onorm_l1, out_w_l1, router_l1, moe_w1_l1, moe_w3_l1, moe_w2_l1):
    params = [
        dict(ada_w=ada_w_l0, ada_b=ada_b_l0, norm1=norm1_l0, norm2=norm2_l0, in_w=in_w_l0,
             gla_w2=gla_w2_l0, gla_b=gla_b_l0, gla_onorm=gla_onorm_l0, sink=sink_l0,
             out_w=out_w_l0, ffn_w1=ffn_w1_l0, ffn_w3=ffn_w3_l0, ffn_w2=ffn_w2_l0),
        dict(ada_w=ada_w_l1, ada_b=ada_b_l1, norm1=norm1_l1, norm2=norm2_l1, in_w=in_w_l1,
             conv_w=conv_w_l1, a_log=a_log_l1, dt_bias=dt_bias_l1, dn_onorm=dn_onorm_l1,
             out_w=out_w_l1, router=router_l1, moe_w1=moe_w1_l1, moe_w3=moe_w3_l1, moe_w2=moe_w2_l1),
    ]
    caches = [(state_gla_l0, cache_k_l0, cache_v_l0), (state_dn_l1,)]

    xp = x_prompt
    new_ctx = []
    for l in range(DEPTH):
        xp, ctx_l = _layer(xp, c_ctx, l, params[l], None)
        new_ctx.append(ctx_l)
    y_prompt = _rmsnorm(xp, final_norm)

    xs = x_sample
    for l in range(DEPTH):
        xs, _ = _layer(xs, c, l, params[l], caches[l])
    y_sample = _rmsnorm(xs, final_norm)

    new_state_gla_l0, new_k_l0, new_v_l0 = new_ctx[0]
    new_state_dn_l1 = new_ctx[1][0]
    return (y_prompt, y_sample, new_state_gla_l0, new_k_l0, new_v_l0, new_state_dn_l1)
```

```python
import functools
import math

import jax
import jax.numpy as jnp
from jax import lax
from jax.experimental import pallas as pl
from jax.experimental.pallas import tpu as pltpu

F32 = jnp.float32
BF16 = jnp.bfloat16

D_MODEL = 1024
N_CTX_SEQ = 16
CTX_LEN = 256
N_LAT_SEQ = 8
LAT_LEN = 2048
N_CTX_TOK = N_CTX_SEQ * CTX_LEN
N_LAT_TOK = N_LAT_SEQ * LAT_LEN
N_TOK = N_CTX_TOK + N_LAT_TOK
GRID_W = 64
EPS = 1e-6
N_MOD = 6

GLA_HEADS = 4
GLA_DK = 64
GLA_DV = 128
GLA_RANK = 16
GLA_TAU = 16.0
CHUNK = 64

SWA_HEADS = 8
SWA_KV_HEADS = 2
SWA_GROUP = SWA_HEADS // SWA_KV_HEADS
SWA_HD = 64
SWA_BLOCK = 128
ROPE_BASE = 10000.0

DN_HEADS = 8
DN_DK = 128
DN_DV = 128
DN_CONV = 5
DN_QKV = 3 * DN_HEADS * DN_DK

D_FF_DENSE = 2816
N_EXPERTS = 8
D_FF_EXPERT = 3584

LANES = 128
SUBLANES = 8
MXU_DIM = 256
VMEM_LIMIT = 56 * 1024 * 1024

TM = 512
N_TILES = N_TOK // TM
CTX_TILES = N_CTX_TOK // TM
LAT_TILES_PER_SEQ = LAT_LEN // TM
FF_CHUNK = MXU_DIM
MOE_TM = 512
MOE_FF = 512
MOE_ROWS = 2 * N_TOK + N_EXPERTS * MOE_TM
MOE_TILES = MOE_ROWS // MOE_TM
MOE_FF_STEPS = D_FF_EXPERT // MOE_FF


def _params(**kw):
    return pltpu.CompilerParams(vmem_limit_bytes=VMEM_LIMIT, **kw)


def _sigmoid(x):
    return 1.0 / (1.0 + jnp.exp(-x))


def _silu(x):
    return x * _sigmoid(x)


def _softplus(x):
    return jnp.maximum(x, 0.0) + jnp.log(1.0 + jnp.exp(-jnp.abs(x)))


def _rms(x, g):
    return x * lax.rsqrt(jnp.mean(x * x, axis=-1, keepdims=True) + EPS) * g


def _dot(a, b):
    return jnp.dot(a, b, preferred_element_type=F32)


def _dot_nt(a, b):
    return lax.dot_general(a, b, (((1,), (1,)), ((), ())), preferred_element_type=F32)


def _dot_tn(a, b):
    return lax.dot_general(a, b, (((0,), (0,)), ((), ())), preferred_element_type=F32)


def _split3(x):
    hi = x.astype(BF16)
    r = x - hi.astype(F32)
    mid = r.astype(BF16)
    lo = (r - mid.astype(F32)).astype(BF16)
    return hi, mid, lo


def _mask_dot(mask_bf, x):
    hi, mid, lo = _split3(x)
    return (_dot(mask_bf, hi) + _dot(mask_bf, mid)) + _dot(mask_bf, lo)


def _dot_mask(x, mask_bf):
    hi, mid, lo = _split3(x)
    return (_dot(hi, mask_bf) + _dot(mid, mask_bf)) + _dot(lo, mask_bf)


def _tri_masks(n):
    r = lax.broadcasted_iota(jnp.int32, (n, n), 0)
    c = lax.broadcasted_iota(jnp.int32, (n, n), 1)
    return c <= r, c >= r


def _mod_index(i):
    return jnp.where(i < CTX_TILES, 0, 1 + (i - CTX_TILES) // LAT_TILES_PER_SEQ)


def _mod_kernel(c_ref, w_ref, b_ref, o_ref):
    s = _silu(c_ref[...]).astype(BF16)
    o_ref[...] = _dot(s, w_ref[...].astype(BF16)) + b_ref[...]


def _modulation(cond, w, b):
    m, d = cond.shape
    n = w.shape[1]
    tn = n // 4
    return pl.pallas_call(
        _mod_kernel,
        out_shape=jax.ShapeDtypeStruct((m, n), F32),
        grid=(n // tn,),
        in_specs=[pl.BlockSpec((m, d), lambda j: (0, 0)),
                  pl.BlockSpec((d, tn), lambda j: (0, j)),
                  pl.BlockSpec((1, tn), lambda j: (0, j))],
        out_specs=pl.BlockSpec((m, tn), lambda j: (0, j)),
        compiler_params=_params(),
        name="modulation",
    )(cond, w, b.reshape(1, n))


L0_GQ, L0_GK, L0_GV, L0_GR, L0_SQ, L0_SK, L0_SV, L0_GLR, L0_END = (
    0, 256, 512, 1024, 1536, 2048, 2176, 2304, 2432)


def _rope(x, cos, sin):
    n = x.shape[-1]
    lane = lax.broadcasted_iota(jnp.int32, x.shape, 1)
    first = (lane % 32) < 16
    partner = jnp.where(first, pltpu.roll(x, n - 16, 1), pltpu.roll(x, 16, 1))
    return x * cos + partner * sin


def _log_sigmoid(x):
    return jnp.minimum(x, 0.0) - jnp.log(1.0 + jnp.exp(-jnp.abs(x)))


def _l0_in_kernel(x_ref, mod_ref, n1_ref, w_ref, w2f_ref, w2b_ref, gb_ref, cos_ref, sin_ref,
                  gq_ref, gk_ref, gv_ref, gr_ref, laf_ref, lab_ref, sq_ref, sk_ref, sv_ref):
    m = mod_ref[0]
    h = _rms(x_ref[...], n1_ref[...]) * (1.0 + m[1:2]) + m[0:1]
    hb = h.astype(BF16)

    def proj(a, b):
        return _dot(hb, w_ref[:, a:b])

    gq_ref[...] = proj(L0_GQ, L0_GK) * (GLA_DK ** -0.5)
    gk_ref[...] = proj(L0_GK, L0_GV)
    gv_ref[...] = proj(L0_GV, L0_GR)
    gr_ref[...] = proj(L0_GR, L0_SQ)
    cos = cos_ref[...]
    sin = sin_ref[...]
    sq_ref[...] = _rope(proj(L0_SQ, L0_SK), cos, sin)
    nk = L0_SV - L0_SK
    sk_ref[...] = _rope(proj(L0_SK, L0_SV), cos[:, :nk], sin[:, :nk])
    sv_ref[...] = proj(L0_SV, L0_GLR)
    glr = proj(L0_GLR, L0_END).astype(BF16)
    gb = gb_ref[...]
    laf_ref[...] = _log_sigmoid(_dot(glr, w2f_ref[...]) + gb[0:1]) * (1.0 / GLA_TAU)
    lab_ref[...] = _log_sigmoid(_dot(glr, w2b_ref[...]) + gb[1:2]) * (1.0 / GLA_TAU)


def _rope_index(i):
    return jnp.where(i < CTX_TILES, LAT_TILES_PER_SEQ, (i - CTX_TILES) % LAT_TILES_PER_SEQ)


def _l0_in(x, mod, norm1, w, w2f, w2b, gb, cos, sin):
    row = lambda n: pl.BlockSpec((TM, n), lambda i: (i, 0))
    full = lambda a: pl.BlockSpec(a.shape, lambda i: (0,) * a.ndim)
    widths = (256, 256, 512, 512, 256, 256, 512, 128, 128)
    return pl.pallas_call(
        _l0_in_kernel,
        out_shape=[jax.ShapeDtypeStruct((N_TOK, n), F32) for n in widths],
        grid=(N_TILES,),
        in_specs=[row(D_MODEL),
                  pl.BlockSpec((1, N_MOD, D_MODEL), lambda i: (_mod_index(i), 0, 0)),
                  full(norm1), full(w), full(w2f), full(w2b), full(gb),
                  pl.BlockSpec((TM, 512), lambda i: (_rope_index(i), 0)),
                  pl.BlockSpec((TM, 512), lambda i: (_rope_index(i), 0))],
        out_specs=[row(n) for n in widths],
        compiler_params=_params(),
        name="l0_in_proj",
    )(x, mod, norm1, w, w2f, w2b, gb, cos, sin)


def _gla_chunk(q, k, v, g, st, mask_bf, mask, tot_row):
    b = _mask_dot(mask_bf, g)
    tot = b[tot_row:tot_row + 1]
    q_dec = (q * jnp.exp(b)).astype(BF16)
    k_inv = (k * jnp.exp(-b)).astype(BF16)
    k_end = (k * jnp.exp(tot - b)).astype(BF16)
    attn = jnp.where(mask, _dot_nt(q_dec, k_inv), 0.0).astype(BF16)
    vb = v.astype(BF16)
    o = _dot(attn, vb) + _dot_nt(q_dec, st.astype(BF16))
    st_new = st * jnp.exp(tot) + _dot_tn(vb, k_end)
    return o, st_new


def _gla_kernel(q_ref, k_ref, v_ref, r_ref, laf_ref, lab_ref, s0_ref, on_ref,
                o_ref, st_ref, oacc_ref, *, seq_len):
    n = seq_len // CHUNK
    lower, upper = _tri_masks(CHUNK)
    lower_bf = lower.astype(BF16)
    upper_bf = upper.astype(BF16)
    oacc_ref[...] = jnp.zeros_like(oacc_ref)
    st_ref[...] = s0_ref[...]

    def body(c, carry):
        rf = pl.multiple_of(c * CHUNK, CHUNK)
        rb = pl.multiple_of((n - 1 - c) * CHUNK, CHUNK)
        for hh in range(2):
            ck = slice(hh * GLA_DK, (hh + 1) * GLA_DK)
            cv = slice(hh * GLA_DV, (hh + 1) * GLA_DV)
            for d, (r0, la_ref, mbf, mb, tr) in enumerate((
                    (rf, laf_ref, lower_bf, lower, CHUNK - 1),
                    (rb, lab_ref, upper_bf, upper, 0))):
                rows = pl.ds(r0, CHUNK)
                o, st_new = _gla_chunk(q_ref[rows, ck], k_ref[rows, ck], v_ref[rows, cv],
                                       la_ref[rows, ck], st_ref[0, d, hh], mbf, mb, tr)
                st_ref[0, d, hh] = st_new
                oacc_ref[rows, cv] += o
        return carry

    lax.fori_loop(0, n, body, 0)
    for hh in range(2):
        cv = slice(hh * GLA_DV, (hh + 1) * GLA_DV)
        o_ref[:, cv] = _rms(oacc_ref[:, cv], on_ref[...]) * _silu(r_ref[:, cv])


def _gla(gq, gk, gv, gr, laf, lab, s0t, onorm, *, seq_len, n_seq, row_off):
    off = row_off // seq_len
    blk = lambda w: pl.BlockSpec((seq_len, w), lambda s, p: (s + off, p))
    st_spec = pl.BlockSpec((1, 2, 2, GLA_DV, GLA_DK), lambda s, p: (s, 0, p, 0, 0))
    return pl.pallas_call(
        functools.partial(_gla_kernel, seq_len=seq_len),
        out_shape=[jax.ShapeDtypeStruct((n_seq * seq_len, GLA_HEADS * GLA_DV), F32),
                   jax.ShapeDtypeStruct((n_seq, 2, GLA_HEADS, GLA_DV, GLA_DK), F32)],
        grid=(n_seq, GLA_HEADS // 2),
        in_specs=[blk(2 * GLA_DK), blk(2 * GLA_DK), blk(2 * GLA_DV), blk(2 * GLA_DV),
                  blk(2 * GLA_DK), blk(2 * GLA_DK), st_spec,
                  pl.BlockSpec((1, GLA_DV), lambda s, p: (0, 0))],
        out_specs=[pl.BlockSpec((seq_len, 2 * GLA_DV), lambda s, p: (s, p)), st_spec],
        scratch_shapes=[pltpu.VMEM((seq_len, 2 * GLA_DV), F32)],
        compiler_params=_params(),
        name=f"gla_{seq_len}",
    )(gq, gk, gv, gr, laf, lab, s0t, onorm)


def _attend(q, kcat, vcat, mask, sink_ref):
    outs = []
    for h in range(SWA_HEADS):
        j = h // SWA_GROUP
        cj = slice(j * SWA_HD, (j + 1) * SWA_HD)
        s = _dot_nt(q[:, h * SWA_HD:(h + 1) * SWA_HD].astype(BF16), kcat[:, cj])
        if mask is not None:
            s = jnp.where(mask, s, -jnp.inf)
        sink = sink_ref[h]
        m = jnp.maximum(jnp.max(s, axis=-1, keepdims=True), sink)
        p = jnp.exp(s - m)
        denom = jnp.sum(p, axis=-1, keepdims=True) + jnp.exp(sink - m)
        outs.append(_dot(p.astype(BF16), vcat[:, cj]) / denom)
    return jnp.concatenate(outs, axis=-1)


def _swa_ctx_kernel(sink_ref, q_ref, k_ref, v_ref, o_ref):
    o_ref[...] = _attend(q_ref[...] * (SWA_HD ** -0.5), k_ref[...].astype(BF16),
                         v_ref[...].astype(BF16), None, sink_ref)


def _swa_ctx(sink, sq, sk, sv):
    blk = lambda w: pl.BlockSpec((CTX_LEN, w), lambda b: (b, 0))
    return pl.pallas_call(
        _swa_ctx_kernel,
        out_shape=jax.ShapeDtypeStruct((N_CTX_TOK, SWA_HEADS * SWA_HD), F32),
        grid=(N_CTX_SEQ,),
        in_specs=[pl.BlockSpec(memory_space=pltpu.SMEM), blk(512), blk(128), blk(128)],
        out_specs=blk(512),
        compiler_params=_params(),
        name="swa_ctx",
    )(sink, sq, sk, sv)


def _swa_lat_kernel(sink_ref, q_ref, kc_ref, vc_ref, kp_ref, kn_ref, k_ref, vp_ref, vn_ref, v_ref,
                    o_ref):
    i = pl.program_id(1)
    nb = pl.num_programs(1)
    kcat = jnp.concatenate([kc_ref[0], kp_ref[...], k_ref[...], kn_ref[...]], axis=0).astype(BF16)
    vcat = jnp.concatenate([vc_ref[0], vp_ref[...], v_ref[...], vn_ref[...]], axis=0).astype(BF16)
    n_ctx = kc_ref.shape[1]
    n = n_ctx + 3 * SWA_BLOCK
    r = lax.broadcasted_iota(jnp.int32, (SWA_BLOCK, n), 0)
    c = lax.broadcasted_iota(jnp.int32, (SWA_BLOCK, n), 1) - n_ctx
    key_pos = (i - 1) * SWA_BLOCK + c
    dist = SWA_BLOCK + r - c
    mask = (c < 0) | ((jnp.abs(dist) <= SWA_BLOCK) & (key_pos >= 0) & (key_pos < nb * SWA_BLOCK))
    o_ref[...] = _attend(q_ref[...] * (SWA_HD ** -0.5), kcat, vcat, mask, sink_ref)


def _swa_lat(sink, sq, sk, sv, cache_k, cache_v):
    nb = LAT_LEN // SWA_BLOCK
    off = N_CTX_TOK // SWA_BLOCK
    cur = lambda b, i: (off + b * nb + i, 0)
    prev = lambda b, i: (off + b * nb + jnp.maximum(i - 1, 0), 0)
    nxt = lambda b, i: (off + b * nb + jnp.minimum(i + 1, nb - 1), 0)
    kv = lambda f: pl.BlockSpec((SWA_BLOCK, 128), f)
    cache = pl.BlockSpec((1, cache_k.shape[1], 128), lambda b, i: (b, 0, 0))
    return pl.pallas_call(
        _swa_lat_kernel,
        out_shape=jax.ShapeDtypeStruct((N_LAT_TOK, SWA_HEADS * SWA_HD), F32),
        grid=(N_LAT_SEQ, nb),
        in_specs=[pl.BlockSpec(memory_space=pltpu.SMEM),
                  pl.BlockSpec((SWA_BLOCK, 512), cur), cache, cache,
                  kv(prev), kv(nxt), kv(cur), kv(prev), kv(nxt), kv(cur)],
        out_specs=pl.BlockSpec((SWA_BLOCK, 512), lambda b, i: (b * nb + i, 0)),
        compiler_params=_params(),
        name="swa_lat",
    )(sink, sq, cache_k, cache_v, sk, sk, sk, sv, sv, sv)


def _l0_out_kernel(og_ref, os_ref, w_ref, x_ref, mod_ref, n2_ref, x1_ref, h2_ref):
    m = mod_ref[0]
    half = og_ref.shape[1]
    y = _dot(og_ref[...].astype(BF16), w_ref[:half]) + _dot(os_ref[...].astype(BF16), w_ref[half:])
    x1 = x_ref[...] + m[2:3] * y
    x1_ref[...] = x1
    h2_ref[...] = (_rms(x1, n2_ref[...]) * (1.0 + m[4:5]) + m[3:4]).astype(BF16)


def _l0_out(o_gla, o_swa, w, x, mod, norm2):
    row = lambda n: pl.BlockSpec((TM, n), lambda i: (i, 0))
    full = lambda a: pl.BlockSpec(a.shape, lambda i: (0,) * a.ndim)
    return pl.pallas_call(
        _l0_out_kernel,
        out_shape=[jax.ShapeDtypeStruct((N_TOK, D_MODEL), F32),
                   jax.ShapeDtypeStruct((N_TOK, D_MODEL), BF16)],
        grid=(N_TILES,),
        in_specs=[row(512), row(512), full(w), row(D_MODEL),
                  pl.BlockSpec((1, N_MOD, D_MODEL), lambda i: (_mod_index(i), 0, 0)), full(norm2)],
        out_specs=[row(D_MODEL), row(D_MODEL)],
        compiler_params=_params(),
        name="l0_out_proj",
    )(o_gla, o_swa, w, x, mod, norm2)


def _ffn_kernel(h_ref, x_ref, w1_ref, w3_ref, w2_ref, mod0_ref, mod1_ref, n1_ref, x2_ref, h3_ref):
    h = h_ref[...]
    acc = jnp.zeros((h.shape[0], D_MODEL), F32)
    for j in range(w1_ref.shape[0]):
        g = (_silu(_dot(h, w1_ref[j])) * _dot(h, w3_ref[j])).astype(BF16)
        acc = acc + _dot(g, w2_ref[j])
    m0 = mod0_ref[0]
    m1 = mod1_ref[0]
    x2 = x_ref[...] + m0[5:6] * acc
    x2_ref[...] = x2
    h3_ref[...] = (_rms(x2, n1_ref[...]) * (1.0 + m1[1:2]) + m1[0:1]).astype(BF16)


def _ffn(h2, x1, w1, w3, w2, mod0, mod1, norm1_next):
    row = lambda n: pl.BlockSpec((TM, n), lambda i: (i, 0))
    full = lambda a: pl.BlockSpec(a.shape, lambda i: (0,) * a.ndim)
    mod = pl.BlockSpec((1, N_MOD, D_MODEL), lambda i: (_mod_index(i), 0, 0))
    return pl.pallas_call(
        _ffn_kernel,
        out_shape=[jax.ShapeDtypeStruct((N_TOK, D_MODEL), F32),
                   jax.ShapeDtypeStruct((N_TOK, D_MODEL), BF16)],
        grid=(N_TILES,),
        in_specs=[row(D_MODEL), row(D_MODEL), full(w1), full(w3), full(w2), mod, mod,
                  full(norm1_next)],
        out_specs=[row(D_MODEL), row(D_MODEL)],
        compiler_params=_params(),
        name="dense_ffn",
    )(h2, x1, w1, w3, w2, mod0, mod1, norm1_next)


L1_Z = DN_QKV
L1_AB = DN_QKV + DN_HEADS * DN_DV
L1_END = L1_AB + LANES
N_GATE = 4 * DN_HEADS


def _dn_gates(ab, neg_a, dt_bias, is_decay):
    return jnp.where(is_decay, neg_a * _softplus(ab + dt_bias), _sigmoid(ab))


def _l1_in_kernel(h_ref, w_ref, wt_ref, alog_ref, dtb_ref, alogc_ref, dtbc_ref,
                  qkv_ref, z_ref, g_ref, gt_ref):
    h = h_ref[...]
    qkv_ref[...] = _dot(h, w_ref[:, :L1_Z])
    z_ref[...] = _dot(h, w_ref[:, L1_Z:L1_AB])
    ab = _dot(h, w_ref[:, L1_AB:L1_END])
    lane = lax.broadcasted_iota(jnp.int32, ab.shape, 1)
    g_ref[...] = _dn_gates(ab, -jnp.exp(alog_ref[...]), dtb_ref[...], lane < 2 * DN_HEADS)
    abt = _dot_nt(wt_ref[...], h)
    row = lax.broadcasted_iota(jnp.int32, abt.shape, 0)
    gt_ref[...] = _dn_gates(abt, -jnp.exp(alogc_ref[...]), dtbc_ref[...], row < 2 * DN_HEADS)


def _l1_in(h3, w, wt, alog, dtb, alogc, dtbc):
    row = lambda n: pl.BlockSpec((TM, n), lambda i: (i, 0))
    full = lambda a: pl.BlockSpec(a.shape, lambda i: (0,) * a.ndim)
    return pl.pallas_call(
        _l1_in_kernel,
        out_shape=[jax.ShapeDtypeStruct((N_TOK, DN_QKV), F32),
                   jax.ShapeDtypeStruct((N_TOK, DN_HEADS * DN_DV), F32),
                   jax.ShapeDtypeStruct((N_TOK, LANES), F32),
                   jax.ShapeDtypeStruct((N_GATE, N_TOK), F32)],
        grid=(N_TILES,),
        in_specs=[row(D_MODEL), full(w), full(wt), full(alog), full(dtb), full(alogc), full(dtbc)],
        out_specs=[row(DN_QKV), row(DN_HEADS * DN_DV), row(LANES),
                   pl.BlockSpec((N_GATE, TM), lambda i: (0, i))],
        compiler_params=_params(),
        name="l1_in_proj",
    )(h3, w, wt, alog, dtb, alogc, dtbc)


HALO = SUBLANES
SOLVE_BASE = 16


def _unit_tri_inverse(a, blk16, blk32):
    x = jnp.where(blk16, -a, 0.0)
    xb = x.astype(BF16)
    x2 = _dot(xb, xb)
    x2b = x2.astype(BF16)
    z = _dot(jnp.concatenate([x2b, xb], axis=0), x2b)
    x4 = z[:CHUNK]
    s = (x + x2) + z[CHUNK:]
    x4b = x4.astype(BF16)
    z = _dot(jnp.concatenate([x4b, s.astype(BF16)], axis=0), x4b)
    s = (s + x4) + z[CHUNK:]
    x8 = z[:CHUNK]
    s = (s + x8) + _dot(s.astype(BF16), x8.astype(BF16))
    for off in (jnp.where(blk32 & ~blk16, a, 0.0), jnp.where(blk32, 0.0, a)):
        sb = s.astype(BF16)
        y = off + _dot(sb, off.astype(BF16))
        s = s - (y + _dot(y.astype(BF16), sb))
    return s


def _dn_kernel(q_ref, k_ref, v_ref, cwq_ref, cwk_ref, cwv_ref, g_ref, gt_ref, s0_ref,
               o_ref, st_ref,
               pad_ref, wq_ref, u_ref, qk_ref, ke_ref, dl_ref, *, seq_len):
    n = seq_len // CHUNK
    h = pl.program_id(1)
    lower, upper = _tri_masks(CHUNK)
    strict_lower = lower & ~upper
    strict_upper = upper & ~lower
    lower_bf = lower.astype(BF16)
    upper_bf = upper.astype(BF16)
    ri = lax.broadcasted_iota(jnp.int32, (CHUNK, CHUNK), 0)
    ci = lax.broadcasted_iota(jnp.int32, (CHUNK, CHUNK), 1)
    blk16 = (ri // SOLVE_BASE) == (ci // SOLVE_BASE)
    blk32 = (ri // (2 * SOLVE_BASE)) == (ci // (2 * SOLVE_BASE))

    zero = jnp.zeros((HALO, DN_DK), F32)
    for t, src in enumerate((q_ref, k_ref, v_ref)):
        pad_ref[t, 0:HALO] = zero
        pad_ref[t, HALO + seq_len:2 * HALO + seq_len] = zero
        pad_ref[t, HALO:HALO + seq_len] = src[...]
    o_ref[...] = jnp.zeros_like(o_ref)

    def conv(t, cw_ref, r0):
        acc = None
        for j in range(DN_CONV):
            term = pad_ref[t, pl.ds(r0 + (HALO - DN_CONV // 2 + j), CHUNK)] * cw_ref[j:j + 1]
            acc = term if acc is None else acc + term
        return _silu(acc)

    def l2n(x):
        return x * lax.rsqrt(jnp.sum(x * x, axis=-1, keepdims=True) + EPS)

    lane = lax.broadcasted_iota(jnp.int32, (CHUNK, LANES), 1)

    def gate_col(gblk, col):
        return jnp.sum(jnp.where(lane == col, gblk, 0.0), axis=-1, keepdims=True)

    def prep(c, carry):
        r0 = pl.multiple_of(c * CHUNK, CHUNK)
        rows = pl.ds(r0, CHUNK)
        q = l2n(conv(0, cwq_ref, r0)) * (DN_DK ** -0.5)
        k = l2n(conv(1, cwk_ref, r0))
        v = conv(2, cwv_ref, r0)
        gblk = g_ref[rows]
        for d, (mbf, mtbf, mask, strict, tr) in enumerate((
                (lower_bf, upper_bf, lower, strict_lower, CHUNK - 1),
                (upper_bf, lower_bf, upper, strict_upper, 0))):
            g_col = gate_col(gblk, h + d * DN_HEADS)
            beta = gate_col(gblk, h + (2 + d) * DN_HEADS)
            g_row = gt_ref[pl.ds(c, 1), pl.ds(h + d * DN_HEADS, 1), :][0]
            gc = _mask_dot(mbf, jnp.broadcast_to(g_col, (CHUNK, DN_DK)))
            gr = _dot_mask(jnp.broadcast_to(g_row, (SUBLANES, CHUNK)), mtbf)[0:1]
            decay = jnp.exp(jnp.where(mask, gc[:, :CHUNK] - gr, -jnp.inf))
            k_beta = k * beta
            kq = _dot_nt(jnp.concatenate([k_beta, q], axis=0).astype(BF16), k.astype(BF16))
            a = jnp.where(strict, kq[:CHUNK] * decay, 0.0)
            qk = kq[CHUNK:] * decay
            eg = jnp.exp(gc)
            rhs = jnp.concatenate([v * beta, k_beta * eg], axis=-1)
            s = _unit_tri_inverse(a, blk16, blk32)
            r = rhs + _dot(s.astype(BF16), rhs.astype(BF16))
            tot = gc[tr:tr + 1]
            idx = d * n + c
            u_ref[idx] = r[:, :DN_DV]
            wq_ref[idx] = jnp.concatenate([r[:, DN_DV:], q * eg], axis=0).astype(BF16)
            qk_ref[idx] = qk.astype(BF16)
            ke_ref[idx] = (k * jnp.exp(tot - gc)).astype(BF16)
            dl_ref[idx] = jnp.broadcast_to(jnp.exp(tot), (SUBLANES, DN_DV))
        return carry

    lax.fori_loop(0, n, prep, 0)

    def scan(c, states):
        new = []
        for d, s in enumerate(states):
            cc = c if d == 0 else n - 1 - c
            idx = d * n + cc
            z = _dot(wq_ref[idx], s.astype(BF16))
            v_new = (u_ref[idx] - z[:CHUNK]).astype(BF16)
            o = z[CHUNK:] + _dot(qk_ref[idx], v_new)
            o_ref[pl.ds(pl.multiple_of(cc * CHUNK, CHUNK), CHUNK)] += o
            new.append(s * dl_ref[idx][0:1] + _dot_tn(ke_ref[idx], v_new))
        return tuple(new)

    s_f, s_b = lax.fori_loop(0, n, scan, (s0_ref[0, 0, 0], s0_ref[0, 1, 0]))
    st_ref[0, 0, 0] = s_f
    st_ref[0, 1, 0] = s_b


def _deltanet(qkv, conv_w, gates, gates_t, s0, *, seq_len, n_seq, row_off):
    off = row_off // seq_len
    n = seq_len // CHUNK
    col = lambda t: pl.BlockSpec((seq_len, DN_DK), lambda s, h: (s + off, t * DN_HEADS + h))
    cw = lambda t: pl.BlockSpec((DN_CONV, DN_DK), lambda s, h: (0, t * DN_HEADS + h))
    st_spec = pl.BlockSpec((1, 2, 1, DN_DK, DN_DV), lambda s, h: (s, 0, h, 0, 0))
    return pl.pallas_call(
        functools.partial(_dn_kernel, seq_len=seq_len),
        out_shape=[jax.ShapeDtypeStruct((n_seq * seq_len, DN_HEADS * DN_DV), F32),
                   jax.ShapeDtypeStruct((n_seq, 2, DN_HEADS, DN_DK, DN_DV), F32)],
        grid=(n_seq, DN_HEADS),
        in_specs=[col(0), col(1), col(2), cw(0), cw(1), cw(2),
                  pl.BlockSpec((seq_len, LANES), lambda s, h: (s + off, 0)),
                  pl.BlockSpec((n, N_GATE, CHUNK), lambda s, h: (s + off, 0, 0)),
                  st_spec],
        out_specs=[pl.BlockSpec((seq_len, DN_DV), lambda s, h: (s, h)), st_spec],
        scratch_shapes=[pltpu.VMEM((3, seq_len + 2 * HALO, DN_DK), F32),
                        pltpu.VMEM((2 * n, 2 * CHUNK, DN_DK), BF16),
                        pltpu.VMEM((2 * n, CHUNK, DN_DV), F32),
                        pltpu.VMEM((2 * n, CHUNK, CHUNK), BF16),
                        pltpu.VMEM((2 * n, CHUNK, DN_DK), BF16),
                        pltpu.VMEM((2 * n, SUBLANES, DN_DV), F32)],
        compiler_params=_params(),
        name=f"deltanet_{seq_len}",
    )(qkv, qkv, qkv, conv_w, conv_w, conv_w, gates, gates_t, s0)


def _l1_out_kernel(o_ref, z_ref, on_ref, w_ref, x_ref, mod_ref, n2_ref, rh_ref, rl_ref,
                   x3_ref, h4_ref, route_ref):
    m = mod_ref[0]
    on = on_ref[...]
    parts = []
    for hd in range(DN_HEADS):
        cs = slice(hd * DN_DV, (hd + 1) * DN_DV)
        parts.append((_rms(o_ref[:, cs], on) * _silu(z_ref[:, cs])).astype(BF16))
    y = _dot(jnp.concatenate(parts, axis=-1), w_ref[...])
    x3 = x_ref[...] + m[2:3] * y
    x3_ref[...] = x3
    h4 = _rms(x3, n2_ref[...]) * (1.0 + m[4:5]) + m[3:4]
    h4_ref[...] = h4
    h4_hi = h4.astype(BF16)
    h4_lo = (h4 - h4_hi.astype(F32)).astype(BF16)
    logits = (_dot(h4_hi, rh_ref[...]) + _dot(h4_hi, rl_ref[...])) + _dot(h4_lo, rh_ref[...])
    lane = lax.broadcasted_iota(jnp.int32, logits.shape, 1).astype(F32)
    logits = jnp.where(lane < N_EXPERTS, logits, -jnp.inf)
    m1 = jnp.max(logits, axis=-1, keepdims=True)
    i1 = jnp.min(jnp.where(logits == m1, lane, float(LANES)), axis=-1, keepdims=True)
    rest = jnp.where(lane == i1, -jnp.inf, logits)
    m2 = jnp.max(rest, axis=-1, keepdims=True)
    i2 = jnp.min(jnp.where(rest == m2, lane, float(LANES)), axis=-1, keepdims=True)
    e2 = jnp.exp(m2 - m1)
    g1 = 1.0 / (1.0 + e2)
    g2 = e2 / (1.0 + e2)
    route_ref[...] = jnp.where(lane == 0, i1, jnp.where(lane == 1, i2, jnp.where(
        lane == 2, g1, jnp.where(lane == 3, g2, 0.0))))


def _l1_out(o_dn, z, onorm, w, x2, mod, norm2, r_hi, r_lo):
    row = lambda n: pl.BlockSpec((TM, n), lambda i: (i, 0))
    full = lambda a: pl.BlockSpec(a.shape, lambda i: (0,) * a.ndim)
    return pl.pallas_call(
        _l1_out_kernel,
        out_shape=[jax.ShapeDtypeStruct((N_TOK, D_MODEL), F32),
                   jax.ShapeDtypeStruct((N_TOK, D_MODEL), F32),
                   jax.ShapeDtypeStruct((N_TOK, LANES), F32)],
        grid=(N_TILES,),
        in_specs=[row(D_MODEL), row(D_MODEL), full(onorm), full(w), row(D_MODEL),
                  pl.BlockSpec((1, N_MOD, D_MODEL), lambda i: (_mod_index(i), 0, 0)),
                  full(norm2), full(r_hi), full(r_lo)],
        out_specs=[row(D_MODEL), row(D_MODEL), row(LANES)],
        compiler_params=_params(),
        name="l1_out_proj_router",
    )(o_dn, z, onorm, w, x2, mod, norm2, r_hi, r_lo)


def _moe_kernel(te_ref, nt_ref, src_ref, dst_ref, h_hbm, w1_ref, w3_ref, w2_ref, y_hbm,
                xs_ref, xb_ref, acc_ref, yb_ref, gsem, ssem):
    i = pl.program_id(0)
    j = pl.program_id(1)
    n_used = nt_ref[0]
    slot = i % 2

    def gather(tile, sl, start):
        def body(r, carry):
            cp = pltpu.make_async_copy(h_hbm.at[pl.ds(src_ref[tile * MOE_TM + r], 1)],
                                       xs_ref.at[sl, pl.ds(r, 1)], gsem.at[sl])
            cp.start() if start else cp.wait()
            return carry
        lax.fori_loop(0, MOE_TM, body, 0, unroll=8)

    def scatter(tile, sl, start):
        def body(r, carry):
            dst = dst_ref[tile * MOE_TM + r]

            @pl.when(dst >= 0)
            def _():
                cp = pltpu.make_async_copy(yb_ref.at[sl, pl.ds(r, 1)],
                                           y_hbm.at[pl.ds(dst, 1)], ssem.at[sl])
                cp.start() if start else cp.wait()
            return carry
        lax.fori_loop(0, MOE_TM, body, 0, unroll=8)

    @pl.when((i < n_used) & (j == 0))
    def _():
        @pl.when(i == 0)
        def _():
            gather(i, slot, True)
        gather(i, slot, False)

        @pl.when(i + 1 < n_used)
        def _():
            gather(i + 1, 1 - slot, True)
        xb_ref[...] = xs_ref[slot].astype(BF16)
        acc_ref[...] = jnp.zeros_like(acc_ref)

    @pl.when(i < n_used)
    def _():
        x = xb_ref[...]
        g = (_silu(_dot(x, w1_ref[0])) * _dot(x, w3_ref[0])).astype(BF16)
        acc_ref[...] += _dot(g, w2_ref[0])

    @pl.when((i < n_used) & (j == pl.num_programs(1) - 1))
    def _():
        @pl.when(i >= 2)
        def _():
            scatter(i - 2, slot, False)
        yb_ref[slot] = acc_ref[...]
        scatter(i, slot, True)

    @pl.when((i == pl.num_programs(0) - 1) & (j == pl.num_programs(1) - 1))
    def _():
        @pl.when(n_used >= 2)
        def _():
            scatter(n_used - 2, n_used % 2, False)
        scatter(n_used - 1, (n_used - 1) % 2, False)


def _moe(tile_expert, n_used, src_rows, dst_rows, h4, w1, w3, w2):
    def wmap(i, j, te, nt, src, dst):
        return (te[i], 0, jnp.where(i < nt[0], j, MOE_FF_STEPS - 1))

    def w2map(i, j, te, nt, src, dst):
        return (te[i], jnp.where(i < nt[0], j, MOE_FF_STEPS - 1), 0)

    grid_spec = pltpu.PrefetchScalarGridSpec(
        num_scalar_prefetch=4,
        grid=(MOE_TILES, MOE_FF_STEPS),
        in_specs=[pl.BlockSpec(memory_space=pl.ANY),
                  pl.BlockSpec((1, D_MODEL, MOE_FF), wmap),
                  pl.BlockSpec((1, D_MODEL, MOE_FF), wmap),
                  pl.BlockSpec((1, MOE_FF, D_MODEL), w2map)],
        out_specs=pl.BlockSpec(memory_space=pl.ANY),
        scratch_shapes=[pltpu.VMEM((2, MOE_TM, D_MODEL), F32),
                        pltpu.VMEM((MOE_TM, D_MODEL), BF16),
                        pltpu.VMEM((MOE_TM, D_MODEL), F32),
                        pltpu.VMEM((2, MOE_TM, D_MODEL), F32),
                        pltpu.SemaphoreType.DMA((2,)),
                        pltpu.SemaphoreType.DMA((2,))])
    return pl.pallas_call(
        _moe_kernel,
        out_shape=jax.ShapeDtypeStruct((2 * N_TOK, D_MODEL), F32),
        grid_spec=grid_spec,
        compiler_params=_params(has_side_effects=True),
        name="routed_experts",
    )(tile_expert, n_used, src_rows, dst_rows, h4, w1, w3, w2)


def _combine_kernel(y0_ref, y1_ref, route_ref, x_ref, mod_ref, fn_ref, o_ref):
    m = mod_ref[0]
    route = route_ref[...]
    y = route[:, 2:3] * y0_ref[...] + route[:, 3:4] * y1_ref[...]
    o_ref[...] = _rms(x_ref[...] + m[5:6] * y, fn_ref[...])


def _combine(y, route, x3, mod, final_norm):
    row = lambda n: pl.BlockSpec((TM, n), lambda i: (i, 0))
    return pl.pallas_call(
        _combine_kernel,
        out_shape=jax.ShapeDtypeStruct((N_TOK, D_MODEL), F32),
        grid=(N_TILES,),
        in_specs=[row(D_MODEL), pl.BlockSpec((TM, D_MODEL), lambda i: (i + N_TILES, 0)),
                  row(LANES), row(D_MODEL),
                  pl.BlockSpec((1, N_MOD, D_MODEL), lambda i: (_mod_index(i), 0, 0)),
                  pl.BlockSpec((1, D_MODEL), lambda i: (0, 0))],
        out_specs=row(D_MODEL),
        compiler_params=_params(),
        name="combine_final_norm",
    )(y, y, route, x3, mod, final_norm)


def _rope_tables():
    rows = LAT_LEN // GRID_W
    row = jnp.repeat(jnp.arange(rows), GRID_W).astype(F32)
    col = (jnp.arange(rows * GRID_W) % GRID_W).astype(F32)
    half = SWA_HD // 2
    inv_freq = ROPE_BASE ** (-jnp.arange(0, half, 2, dtype=F32) / half)
    ang_r = row[:, None] * inv_freq
    ang_c = col[:, None] * inv_freq
    cos = jnp.concatenate([jnp.cos(ang_r)] * 2 + [jnp.cos(ang_c)] * 2, axis=-1)
    sin = jnp.concatenate([-jnp.sin(ang_r), jnp.sin(ang_r), -jnp.sin(ang_c), jnp.sin(ang_c)], axis=-1)
    cos = jnp.tile(cos, (1, SWA_HEADS))
    sin = jnp.tile(sin, (1, SWA_HEADS))
    cos = jnp.concatenate([cos, jnp.ones((TM, cos.shape[1]), F32)], axis=0)
    sin = jnp.concatenate([sin, jnp.zeros((TM, sin.shape[1]), F32)], axis=0)
    return cos, sin


def _dispatch(route):
    e = jnp.concatenate([route[:, 0], route[:, 1]]).astype(jnp.int32)
    onehot = (e[:, None] == jnp.arange(N_EXPERTS)[None, :]).astype(jnp.int32)
    counts = jnp.sum(onehot, axis=0)
    rank = jnp.sum((jnp.cumsum(onehot, axis=0) - onehot) * onehot, axis=1)
    padded = ((counts + MOE_TM - 1) // MOE_TM) * MOE_TM
    ends = jnp.cumsum(padded)
    starts = ends - padded
    pos = starts[e] + rank
    entry = jnp.arange(2 * N_TOK, dtype=jnp.int32)
    src = jnp.zeros((MOE_ROWS,), jnp.int32).at[pos].set(entry % N_TOK)
    dst = jnp.full((MOE_ROWS,), -1, jnp.int32).at[pos].set(entry)
    n_used = (ends[-1] // MOE_TM).astype(jnp.int32)
    tile_start = jnp.arange(MOE_TILES, dtype=jnp.int32) * MOE_TM
    tile_expert = jnp.sum(tile_start[:, None] >= ends[None, :], axis=1).astype(jnp.int32)
    last = jnp.take(tile_expert, jnp.maximum(n_used - 1, 0))
    tile_expert = jnp.where(jnp.arange(MOE_TILES) < n_used, tile_expert, last)
    return tile_expert, n_used.reshape(1), src, dst


def kernel(x_prompt, x_sample, c, state_gla_l0, cache_k_l0, cache_v_l0, state_dn_l1, c_ctx, final_norm, ada_w_l0, ada_b_l0, norm1_l0, norm2_l0, in_w_l0, gla_w2_l0, gla_b_l0, gla_onorm_l0, sink_l0, out_w_l0, ffn_w1_l0, ffn_w3_l0, ffn_w2_l0, ada_w_l1, ada_b_l1, norm1_l1, norm2_l1, in_w_l1, conv_w_l1, a_log_l1, dt_bias_l1, dn_onorm_l1, out_w_l1, router_l1, moe_w1_l1, moe_w3_l1, moe_w2_l1):
    D = D_MODEL
    x = jnp.concatenate([x_prompt.reshape(N_CTX_TOK, D), x_sample.reshape(N_LAT_TOK, D)], axis=0)
    row2 = lambda v: v.reshape(1, -1)

    cond = jnp.concatenate([c_ctx[None], c, jnp.zeros((2 * SUBLANES - 1 - N_LAT_SEQ, D), F32)], axis=0)
    mod0 = _modulation(cond, ada_w_l0, ada_b_l0).reshape(-1, N_MOD, D)
    mod1 = _modulation(cond, ada_w_l1, ada_b_l1).reshape(-1, N_MOD, D)

    gq_w, gk_w, gv_w, gr_w, glrf_w, glrb_w, sq_w, sk_w, sv_w = jnp.split(
        in_w_l0, [256, 512, 1024, 1536, 1552, 1568, 2080, 2208], axis=1)
    w0 = jnp.concatenate([gq_w, gk_w, gv_w, gr_w, sq_w, sk_w, sv_w, glrf_w, glrb_w,
                          jnp.zeros((D, L0_END - L0_GLR - 2 * GLA_RANK), F32)], axis=1).astype(BF16)
    w2f = jnp.zeros((LANES, GLA_HEADS * GLA_DK), F32).at[:GLA_RANK].set(gla_w2_l0[0]).astype(BF16)
    w2b = jnp.zeros((LANES, GLA_HEADS * GLA_DK), F32).at[GLA_RANK:2 * GLA_RANK].set(gla_w2_l0[1]).astype(BF16)
    cos, sin = _rope_tables()
    gq, gk, gv, gr, laf, lab, sq, sk, sv = _l0_in(x, mod0, row2(norm1_l0), w0, w2f, w2b, gla_b_l0, cos, sin)

    onorm0 = row2(gla_onorm_l0)
    zero_state = jnp.zeros((N_CTX_SEQ, 2, GLA_HEADS, GLA_DV, GLA_DK), F32)
    o_gla_c, st_gla = _gla(gq, gk, gv, gr, laf, lab, zero_state, onorm0,
                           seq_len=CTX_LEN, n_seq=N_CTX_SEQ, row_off=0)
    o_gla_l, _ = _gla(gq, gk, gv, gr, laf, lab, jnp.swapaxes(state_gla_l0, -1, -2), onorm0,
                      seq_len=LAT_LEN, n_seq=N_LAT_SEQ, row_off=N_CTX_TOK)
    o_swa_c = _swa_ctx(sink_l0, sq, sk, sv)
    o_swa_l = _swa_lat(sink_l0, sq, sk, sv,
                       cache_k_l0.reshape(N_LAT_SEQ, -1, SWA_KV_HEADS * SWA_HD),
                       cache_v_l0.reshape(N_LAT_SEQ, -1, SWA_KV_HEADS * SWA_HD))
    o_gla = jnp.concatenate([o_gla_c, o_gla_l], axis=0)
    o_swa = jnp.concatenate([o_swa_c, o_swa_l], axis=0)
    x1, h2 = _l0_out(o_gla, o_swa, out_w_l0.astype(BF16), x, mod0, row2(norm2_l0))

    nff = D_FF_DENSE // FF_CHUNK
    w1 = ffn_w1_l0.astype(BF16).reshape(D, nff, FF_CHUNK).transpose(1, 0, 2)
    w3 = ffn_w3_l0.astype(BF16).reshape(D, nff, FF_CHUNK).transpose(1, 0, 2)
    w2 = ffn_w2_l0.astype(BF16).reshape(nff, FF_CHUNK, D)
    x2, h3 = _ffn(h2, x1, w1, w3, w2, mod0, mod1, row2(norm1_l1))

    w_ab = in_w_l1[:, L1_AB:]
    wl1 = jnp.concatenate([in_w_l1, jnp.zeros((D, L1_END - in_w_l1.shape[1]), F32)], axis=1).astype(BF16)
    zeros_g = jnp.zeros((2 * DN_HEADS,), F32)
    alog = jnp.concatenate([a_log_l1.reshape(-1), zeros_g])
    dtb = jnp.concatenate([dt_bias_l1.reshape(-1), zeros_g])
    pad_lane = lambda v: jnp.concatenate([v, jnp.zeros((LANES - N_GATE,), F32)]).reshape(1, LANES)
    qkv, z, gates, gates_t = _l1_in(h3, wl1, w_ab.T.astype(BF16), pad_lane(alog), pad_lane(dtb),
                                    alog.reshape(N_GATE, 1), dtb.reshape(N_GATE, 1))
    gates_t = gates_t.reshape(N_GATE, N_TOK // CHUNK, CHUNK).transpose(1, 0, 2)
    zero_dn = jnp.zeros((N_CTX_SEQ, 2, DN_HEADS, DN_DK, DN_DV), F32)
    o_dn_c, st_dn = _deltanet(qkv, conv_w_l1, gates, gates_t, zero_dn,
                              seq_len=CTX_LEN, n_seq=N_CTX_SEQ, row_off=0)
    o_dn_l, _ = _deltanet(qkv, conv_w_l1, gates, gates_t, state_dn_l1,
                          seq_len=LAT_LEN, n_seq=N_LAT_SEQ, row_off=N_CTX_TOK)
    o_dn = jnp.concatenate([o_dn_c, o_dn_l], axis=0)

    router = jnp.concatenate([router_l1, jnp.zeros((D, LANES - N_EXPERTS), F32)], axis=1)
    r_hi = router.astype(BF16)
    r_lo = (router - r_hi.astype(F32)).astype(BF16)
    x3, h4, route = _l1_out(o_dn, z, row2(dn_onorm_l1), out_w_l1.astype(BF16), x2, mod1,
                            row2(norm2_l1), r_hi, r_lo)

    tile_expert, n_used, src, dst = _dispatch(route)
    y = _moe(tile_expert, n_used, src, dst, h4,
             moe_w1_l1.astype(BF16), moe_w3_l1.astype(BF16), moe_w2_l1.astype(BF16))
    out = _combine(y, route, x3, mod1, row2(final_norm))

    y_prompt = out[:N_CTX_TOK].reshape(N_CTX_SEQ, CTX_LEN, D)
    y_sample = out[N_CTX_TOK:].reshape(N_LAT_SEQ, LAT_LEN, D)
    new_state_gla = jnp.swapaxes(st_gla, -1, -2)
    new_k = sk[:N_CTX_TOK].reshape(N_CTX_SEQ, CTX_LEN, SWA_KV_HEADS, SWA_HD)
    new_v = sv[:N_CTX_TOK].reshape(N_CTX_SEQ, CTX_LEN, SWA_KV_HEADS, SWA_HD)
    return (y_prompt, y_sample, new_state_gla, new_k, new_v, st_dn)
```

```python
import functools
import math

import jax
import jax.numpy as jnp
from jax import lax
from jax.experimental import pallas as pl
from jax.experimental.pallas import tpu as pltpu

F32 = jnp.float32
BF16 = jnp.bfloat16

D_MODEL = 1024
N_CTX_SEQ = 16
CTX_LEN = 256
N_LAT_SEQ = 8
LAT_LEN = 2048
N_CTX_TOK = N_CTX_SEQ * CTX_LEN
N_LAT_TOK = N_LAT_SEQ * LAT_LEN
N_TOK = N_CTX_TOK + N_LAT_TOK
GRID_W = 64
EPS = 1e-6
N_MOD = 6

GLA_HEADS = 4
GLA_DK = 64
GLA_DV = 128
GLA_RANK = 16
GLA_TAU = 16.0
CHUNK = 64
GLA_UNROLL = 4

SWA_HEADS = 8
SWA_KV_HEADS = 2
SWA_GROUP = SWA_HEADS // SWA_KV_HEADS
SWA_HD = 64
SWA_BLOCK = 128
ROPE_BASE = 10000.0

DN_HEADS = 8
DN_DK = 128
DN_DV = 128
DN_CONV = 5
DN_QKV = 3 * DN_HEADS * DN_DK

D_FF_DENSE = 2816
N_EXPERTS = 8
D_FF_EXPERT = 3584

LANES = 128
SUBLANES = 8
MXU_DIM = 256
VMEM_LIMIT = 56 * 1024 * 1024

TM = 512
N_TILES = N_TOK // TM
CTX_TILES = N_CTX_TOK // TM
LAT_TILES_PER_SEQ = LAT_LEN // TM
FF_CHUNK = MXU_DIM
MOE_TM = 512
MOE_FF = 1792
MOE_ROWS = 2 * N_TOK + N_EXPERTS * MOE_TM
MOE_TILES = MOE_ROWS // MOE_TM
MOE_FF_STEPS = D_FF_EXPERT // MOE_FF


def _params(**kw):
    return pltpu.CompilerParams(vmem_limit_bytes=VMEM_LIMIT, **kw)


def _sigmoid(x):
    return 1.0 / (1.0 + jnp.exp(-x))


def _silu(x):
    return x * _sigmoid(x)


def _softplus(x):
    return jnp.maximum(x, 0.0) + jnp.log(1.0 + jnp.exp(-jnp.abs(x)))


def _rms(x, g):
    return x * lax.rsqrt(jnp.mean(x * x, axis=-1, keepdims=True) + EPS) * g


def _dot(a, b):
    return jnp.dot(a, b, preferred_element_type=F32)


def _dot_nt(a, b):
    return lax.dot_general(a, b, (((1,), (1,)), ((), ())), preferred_element_type=F32)


def _dot_tn(a, b):
    return lax.dot_general(a, b, (((0,), (0,)), ((), ())), preferred_element_type=F32)


def _split3(x):
    hi = x.astype(BF16)
    r = x - hi.astype(F32)
    mid = r.astype(BF16)
    lo = (r - mid.astype(F32)).astype(BF16)
    return hi, mid, lo


def _mask_dot(mask_bf, x):
    hi, mid, lo = _split3(x)
    return (_dot(mask_bf, hi) + _dot(mask_bf, mid)) + _dot(mask_bf, lo)


def _dot_mask(x, mask_bf):
    hi, mid, lo = _split3(x)
    return (_dot(hi, mask_bf) + _dot(mid, mask_bf)) + _dot(lo, mask_bf)


def _interleave(chains):
    results = [None] * len(chains)
    active = list(enumerate(chains))
    while active:
        still = []
        for i, chain in active:
            try:
                next(chain)
                still.append((i, chain))
            except StopIteration as stop:
                results[i] = stop.value
        active = still
    return results


def _tri_masks(n):
    r = lax.broadcasted_iota(jnp.int32, (n, n), 0)
    c = lax.broadcasted_iota(jnp.int32, (n, n), 1)
    return c <= r, c >= r


def _ctx_rows(w):
    return pl.BlockSpec((TM, w), lambda i: (jnp.minimum(i, CTX_TILES - 1), 0))


def _lat_rows(w):
    return pl.BlockSpec((TM, w), lambda i: (jnp.maximum(i - CTX_TILES, 0), 0))


def _pick_rows(ctx_ref, lat_ref, cols=slice(None)):
    return jnp.where(pl.program_id(0) < CTX_TILES, ctx_ref[:, cols], lat_ref[:, cols])


def _mod_index(i):
    return jnp.where(i < CTX_TILES, 0, 1 + (i - CTX_TILES) // LAT_TILES_PER_SEQ)


def _mod_kernel(c_ref, w_ref, b_ref, o_ref):
    s = _silu(c_ref[...]).astype(BF16)
    o_ref[...] = _dot(s, w_ref[...].astype(BF16)) + b_ref[...]


def _modulation(cond, w, b):
    m, d = cond.shape
    n = w.shape[1]
    tn = n // 4
    return pl.pallas_call(
        _mod_kernel,
        out_shape=jax.ShapeDtypeStruct((m, n), F32),
        grid=(n // tn,),
        in_specs=[pl.BlockSpec((m, d), lambda j: (0, 0)),
                  pl.BlockSpec((d, tn), lambda j: (0, j)),
                  pl.BlockSpec((1, tn), lambda j: (0, j))],
        out_specs=pl.BlockSpec((m, tn), lambda j: (0, j)),
        compiler_params=_params(),
        name="modulation",
    )(cond, w, b.reshape(1, n))


L0_GQ, L0_GK, L0_GV, L0_GR, L0_SQ, L0_SK, L0_SV, L0_GLR, L0_END = (
    0, 256, 512, 1024, 1536, 2048, 2176, 2304, 2432)


def _rope(x, cos, sin):
    n = x.shape[-1]
    lane = lax.broadcasted_iota(jnp.int32, x.shape, 1)
    first = (lane % 32) < 16
    partner = jnp.where(first, pltpu.roll(x, n - 16, 1), pltpu.roll(x, 16, 1))
    return x * cos + partner * sin


def _log_sigmoid(x):
    return jnp.minimum(x, 0.0) - jnp.log(1.0 + jnp.exp(-jnp.abs(x)))


def _l0_in_kernel(xc_ref, xl_ref, mod_ref, n1_ref, w_ref, w2f_ref, w2b_ref, gb_ref, cos_ref, sin_ref,
                  gq_ref, gk_ref, gv_ref, gr_ref, laf_ref, lab_ref, sq_ref, sk_ref, sv_ref):
    m = mod_ref[0]
    h = _rms(_pick_rows(xc_ref, xl_ref), n1_ref[...]) * (1.0 + m[1:2]) + m[0:1]
    hb = h.astype(BF16)

    def proj(a, b):
        return _dot(hb, w_ref[:, a:b])

    gq_ref[...] = proj(L0_GQ, L0_GK) * (GLA_DK ** -0.5)
    gk_ref[...] = proj(L0_GK, L0_GV)
    gv_ref[...] = proj(L0_GV, L0_GR)
    gr_ref[...] = proj(L0_GR, L0_SQ)
    cos = cos_ref[...]
    sin = sin_ref[...]
    sq_ref[...] = _rope(proj(L0_SQ, L0_SK), cos, sin)
    nk = L0_SV - L0_SK
    sk_ref[...] = _rope(proj(L0_SK, L0_SV), cos[:, :nk], sin[:, :nk])
    sv_ref[...] = proj(L0_SV, L0_GLR)
    glr = proj(L0_GLR, L0_END).astype(BF16)
    gb = gb_ref[...]
    laf_ref[...] = _log_sigmoid(_dot(glr, w2f_ref[...]) + gb[0:1]) * (1.0 / GLA_TAU)
    lab_ref[...] = _log_sigmoid(_dot(glr, w2b_ref[...]) + gb[1:2]) * (1.0 / GLA_TAU)


def _rope_index(i):
    return jnp.where(i < CTX_TILES, LAT_TILES_PER_SEQ, (i - CTX_TILES) % LAT_TILES_PER_SEQ)


def _l0_in(x_ctx, x_lat, mod, norm1, w, w2f, w2b, gb, cos, sin):
    row = lambda n: pl.BlockSpec((TM, n), lambda i: (i, 0))
    full = lambda a: pl.BlockSpec(a.shape, lambda i: (0,) * a.ndim)
    widths = (256, 256, 512, 512, 256, 256, 512, 128, 128)
    return pl.pallas_call(
        _l0_in_kernel,
        out_shape=[jax.ShapeDtypeStruct((N_TOK, n), F32) for n in widths],
        grid=(N_TILES,),
        in_specs=[_ctx_rows(D_MODEL), _lat_rows(D_MODEL),
                  pl.BlockSpec((1, N_MOD, D_MODEL), lambda i: (_mod_index(i), 0, 0)),
                  full(norm1), full(w), full(w2f), full(w2b), full(gb),
                  pl.BlockSpec((TM, 512), lambda i: (_rope_index(i), 0)),
                  pl.BlockSpec((TM, 512), lambda i: (_rope_index(i), 0))],
        out_specs=[row(n) for n in widths],
        compiler_params=_params(),
        name="l0_in_proj",
    )(x_ctx, x_lat, mod, norm1, w, w2f, w2b, gb, cos, sin)


def _gla_kernel(q_ref, k_ref, v_ref, r_ref, laf_ref, lab_ref, s0_ref, on_ref,
                o_ref, st_ref, oacc_ref, *, seq_len):
    n = seq_len // CHUNK
    lower, upper = _tri_masks(CHUNK)
    lower_bf = lower.astype(BF16)
    upper_bf = upper.astype(BF16)
    oacc_ref[...] = jnp.zeros_like(oacc_ref)
    st_ref[...] = s0_ref[...]

    def chain(d, chunks, la_ref, mask_bf, mask, tot_row):
        rows = [pl.ds(pl.multiple_of(c * CHUNK, CHUNK), CHUNK) for c in chunks]
        cums = [_mask_dot(mask_bf, la_ref[r, :]) for r in rows]
        yield
        parts = []
        for r, b2 in zip(rows, cums):
            for hh in range(2):
                ck = slice(hh * GLA_DK, (hh + 1) * GLA_DK)
                cv = slice(hh * GLA_DV, (hh + 1) * GLA_DV)
                b = b2[:, ck]
                tot = b[tot_row:tot_row + 1]
                q = q_ref[r, ck]
                k = k_ref[r, ck]
                q_dec = (q * jnp.exp(b)).astype(BF16)
                k_inv = (k * jnp.exp(-b)).astype(BF16)
                k_end = (k * jnp.exp(tot - b)).astype(BF16)
                vb = v_ref[r, cv].astype(BF16)
                parts.append((r, hh, cv, jnp.exp(tot), vb, q_dec, _dot_nt(q_dec, k_inv), _dot_tn(vb, k_end)))
        yield
        states = [st_ref[0, d, hh] for hh in range(2)]
        outs = []
        for r, hh, cv, decay, vb, q_dec, attn, kv in parts:
            o_intra = _dot(jnp.where(mask, attn, 0.0).astype(BF16), vb)
            o_inter = _dot_nt(q_dec, states[hh].astype(BF16))
            states[hh] = states[hh] * decay + kv
            outs.append((r, cv, o_intra, o_inter))
        yield
        for r, cv, o_intra, o_inter in outs:
            oacc_ref[r, cv] += o_intra + o_inter
        for hh in range(2):
            st_ref[0, d, hh] = states[hh]

    unroll = min(GLA_UNROLL, n)

    def body(g, carry):
        fwd = [g * unroll + t for t in range(unroll)]
        bwd = [n - 1 - c for c in fwd]
        _interleave([chain(0, fwd, laf_ref, lower_bf, lower, CHUNK - 1),
                     chain(1, bwd, lab_ref, upper_bf, upper, 0)])
        return carry

    lax.fori_loop(0, n // unroll, body, 0)
    for hh in range(2):
        cv = slice(hh * GLA_DV, (hh + 1) * GLA_DV)
        o_ref[:, cv] = _rms(oacc_ref[:, cv], on_ref[...]) * _silu(r_ref[:, cv])


def _gla(gq, gk, gv, gr, laf, lab, s0t, onorm, *, seq_len, n_seq, row_off):
    off = row_off // seq_len
    blk = lambda w: pl.BlockSpec((seq_len, w), lambda s, p: (s + off, p))
    st_spec = pl.BlockSpec((1, 2, 2, GLA_DV, GLA_DK), lambda s, p: (s, 0, p, 0, 0))
    return pl.pallas_call(
        functools.partial(_gla_kernel, seq_len=seq_len),
        out_shape=[jax.ShapeDtypeStruct((n_seq * seq_len, GLA_HEADS * GLA_DV), F32),
                   jax.ShapeDtypeStruct((n_seq, 2, GLA_HEADS, GLA_DV, GLA_DK), F32)],
        grid=(n_seq, GLA_HEADS // 2),
        in_specs=[blk(2 * GLA_DK), blk(2 * GLA_DK), blk(2 * GLA_DV), blk(2 * GLA_DV),
                  blk(2 * GLA_DK), blk(2 * GLA_DK), st_spec,
                  pl.BlockSpec((1, GLA_DV), lambda s, p: (0, 0))],
        out_specs=[pl.BlockSpec((seq_len, 2 * GLA_DV), lambda s, p: (s, p)), st_spec],
        scratch_shapes=[pltpu.VMEM((seq_len, 2 * GLA_DV), F32)],
        compiler_params=_params(),
        name=f"gla_{seq_len}",
    )(gq, gk, gv, gr, laf, lab, s0t, onorm)


def _attend(q, kcat, vcat, mask, sink_ref):
    def head(h):
        j = h // SWA_GROUP
        cj = slice(j * SWA_HD, (j + 1) * SWA_HD)
        s = _dot_nt(q[:, h * SWA_HD:(h + 1) * SWA_HD].astype(BF16), kcat[:, cj])
        yield
        if mask is not None:
            s = jnp.where(mask, s, -jnp.inf)
        sink = sink_ref[h]
        m = jnp.maximum(jnp.max(s, axis=-1, keepdims=True), sink)
        p = jnp.exp(s - m)
        denom = jnp.sum(p, axis=-1, keepdims=True) + jnp.exp(sink - m)
        o = _dot(p.astype(BF16), vcat[:, cj])
        yield
        return o / denom

    return jnp.concatenate(_interleave([head(h) for h in range(SWA_HEADS)]), axis=-1)


def _swa_ctx_kernel(sink_ref, q_ref, k_ref, v_ref, o_ref):
    o_ref[...] = _attend(q_ref[...] * (SWA_HD ** -0.5), k_ref[...].astype(BF16),
                         v_ref[...].astype(BF16), None, sink_ref)


def _swa_ctx(sink, sq, sk, sv):
    blk = lambda w: pl.BlockSpec((CTX_LEN, w), lambda b: (b, 0))
    return pl.pallas_call(
        _swa_ctx_kernel,
        out_shape=jax.ShapeDtypeStruct((N_CTX_TOK, SWA_HEADS * SWA_HD), F32),
        grid=(N_CTX_SEQ,),
        in_specs=[pl.BlockSpec(memory_space=pltpu.SMEM), blk(512), blk(128), blk(128)],
        out_specs=blk(512),
        compiler_params=_params(),
        name="swa_ctx",
    )(sink, sq, sk, sv)


def _swa_lat_kernel(sink_ref, q_ref, kc_ref, vc_ref, kp_ref, kn_ref, k_ref, vp_ref, vn_ref, v_ref,
                    o_ref):
    i = pl.program_id(1)
    nb = pl.num_programs(1)
    kcat = jnp.concatenate([kc_ref[0], kp_ref[...], k_ref[...], kn_ref[...]], axis=0).astype(BF16)
    vcat = jnp.concatenate([vc_ref[0], vp_ref[...], v_ref[...], vn_ref[...]], axis=0).astype(BF16)
    n_ctx = kc_ref.shape[1]
    n = n_ctx + 3 * SWA_BLOCK
    r = lax.broadcasted_iota(jnp.int32, (SWA_BLOCK, n), 0)
    c = lax.broadcasted_iota(jnp.int32, (SWA_BLOCK, n), 1) - n_ctx
    key_pos = (i - 1) * SWA_BLOCK + c
    dist = SWA_BLOCK + r - c
    mask = (c < 0) | ((jnp.abs(dist) <= SWA_BLOCK) & (key_pos >= 0) & (key_pos < nb * SWA_BLOCK))
    o_ref[...] = _attend(q_ref[...] * (SWA_HD ** -0.5), kcat, vcat, mask, sink_ref)


def _swa_lat(sink, sq, sk, sv, cache_k, cache_v):
    nb = LAT_LEN // SWA_BLOCK
    off = N_CTX_TOK // SWA_BLOCK
    cur = lambda b, i: (off + b * nb + i, 0)
    prev = lambda b, i: (off + b * nb + jnp.maximum(i - 1, 0), 0)
    nxt = lambda b, i: (off + b * nb + jnp.minimum(i + 1, nb - 1), 0)
    kv = lambda f: pl.BlockSpec((SWA_BLOCK, 128), f)
    cache = pl.BlockSpec((1, cache_k.shape[1], 128), lambda b, i: (b, 0, 0))
    return pl.pallas_call(
        _swa_lat_kernel,
        out_shape=jax.ShapeDtypeStruct((N_LAT_TOK, SWA_HEADS * SWA_HD), F32),
        grid=(N_LAT_SEQ, nb),
        in_specs=[pl.BlockSpec(memory_space=pltpu.SMEM),
                  pl.BlockSpec((SWA_BLOCK, 512), cur), cache, cache,
                  kv(prev), kv(nxt), kv(cur), kv(prev), kv(nxt), kv(cur)],
        out_specs=pl.BlockSpec((SWA_BLOCK, 512), lambda b, i: (b * nb + i, 0)),
        compiler_params=_params(),
        name="swa_lat",
    )(sink, sq, cache_k, cache_v, sk, sk, sk, sv, sv, sv)


def _l0_out_kernel(ogc_ref, ogl_ref, osc_ref, osl_ref, w_ref, xc_ref, xl_ref, mod_ref, n2_ref,
                   x1_ref, h2_ref):
    m = mod_ref[0]
    half = ogc_ref.shape[1]
    y = (_dot(_pick_rows(ogc_ref, ogl_ref).astype(BF16), w_ref[:half])
         + _dot(_pick_rows(osc_ref, osl_ref).astype(BF16), w_ref[half:]))
    x1 = _pick_rows(xc_ref, xl_ref) + m[2:3] * y
    x1_ref[...] = x1
    h2_ref[...] = (_rms(x1, n2_ref[...]) * (1.0 + m[4:5]) + m[3:4]).astype(BF16)


def _l0_out(o_gla_c, o_gla_l, o_swa_c, o_swa_l, w, x_ctx, x_lat, mod, norm2):
    row = lambda n: pl.BlockSpec((TM, n), lambda i: (i, 0))
    full = lambda a: pl.BlockSpec(a.shape, lambda i: (0,) * a.ndim)
    return pl.pallas_call(
        _l0_out_kernel,
        out_shape=[jax.ShapeDtypeStruct((N_TOK, D_MODEL), F32),
                   jax.ShapeDtypeStruct((N_TOK, D_MODEL), BF16)],
        grid=(N_TILES,),
        in_specs=[_ctx_rows(512), _lat_rows(512), _ctx_rows(512), _lat_rows(512), full(w),
                  _ctx_rows(D_MODEL), _lat_rows(D_MODEL),
                  pl.BlockSpec((1, N_MOD, D_MODEL), lambda i: (_mod_index(i), 0, 0)), full(norm2)],
        out_specs=[row(D_MODEL), row(D_MODEL)],
        compiler_params=_params(),
        name="l0_out_proj",
    )(o_gla_c, o_gla_l, o_swa_c, o_swa_l, w, x_ctx, x_lat, mod, norm2)


def _ffn_kernel(h_ref, x_ref, w1_ref, w3_ref, w2_ref, mod0_ref, mod1_ref, n1_ref, x2_ref, h3_ref):
    h = h_ref[...]
    acc = jnp.zeros((h.shape[0], D_MODEL), F32)
    for j in range(w1_ref.shape[0]):
        g = (_silu(_dot(h, w1_ref[j])) * _dot(h, w3_ref[j])).astype(BF16)
        acc = acc + _dot(g, w2_ref[j])
    m0 = mod0_ref[0]
    m1 = mod1_ref[0]
    x2 = x_ref[...] + m0[5:6] * acc
    x2_ref[...] = x2
    h3_ref[...] = (_rms(x2, n1_ref[...]) * (1.0 + m1[1:2]) + m1[0:1]).astype(BF16)


def _ffn(h2, x1, w1, w3, w2, mod0, mod1, norm1_next):
    row = lambda n: pl.BlockSpec((TM, n), lambda i: (i, 0))
    full = lambda a: pl.BlockSpec(a.shape, lambda i: (0,) * a.ndim)
    mod = pl.BlockSpec((1, N_MOD, D_MODEL), lambda i: (_mod_index(i), 0, 0))
    return pl.pallas_call(
        _ffn_kernel,
        out_shape=[jax.ShapeDtypeStruct((N_TOK, D_MODEL), F32),
                   jax.ShapeDtypeStruct((N_TOK, D_MODEL), BF16)],
        grid=(N_TILES,),
        in_specs=[row(D_MODEL), row(D_MODEL), full(w1), full(w3), full(w2), mod, mod,
                  full(norm1_next)],
        out_specs=[row(D_MODEL), row(D_MODEL)],
        compiler_params=_params(),
        name="dense_ffn",
    )(h2, x1, w1, w3, w2, mod0, mod1, norm1_next)


L1_Z = DN_QKV
L1_AB = DN_QKV + DN_HEADS * DN_DV
L1_END = L1_AB + LANES
N_GATE = 4 * DN_HEADS


def _dn_gates(ab, neg_a, dt_bias, is_decay):
    return jnp.where(is_decay, neg_a * _softplus(ab + dt_bias), _sigmoid(ab))


def _l1_in_kernel(h_ref, w_ref, wt_ref, alog_ref, dtb_ref, alogc_ref, dtbc_ref,
                  qkv_ref, z_ref, g_ref, gt_ref):
    h = h_ref[...]
    qkv_ref[...] = _dot(h, w_ref[:, :L1_Z])
    z_ref[...] = _dot(h, w_ref[:, L1_Z:L1_AB])
    ab = _dot(h, w_ref[:, L1_AB:L1_END])
    lane = lax.broadcasted_iota(jnp.int32, ab.shape, 1)
    g_ref[...] = _dn_gates(ab, -jnp.exp(alog_ref[...]), dtb_ref[...], lane < 2 * DN_HEADS)
    abt = _dot_nt(wt_ref[...], h)
    row = lax.broadcasted_iota(jnp.int32, abt.shape, 0)
    gt_ref[...] = _dn_gates(abt, -jnp.exp(alogc_ref[...]), dtbc_ref[...], row < 2 * DN_HEADS)


def _l1_in(h3, w, wt, alog, dtb, alogc, dtbc):
    row = lambda n: pl.BlockSpec((TM, n), lambda i: (i, 0))
    full = lambda a: pl.BlockSpec(a.shape, lambda i: (0,) * a.ndim)
    return pl.pallas_call(
        _l1_in_kernel,
        out_shape=[jax.ShapeDtypeStruct((N_TOK, DN_QKV), F32),
                   jax.ShapeDtypeStruct((N_TOK, DN_HEADS * DN_DV), F32),
                   jax.ShapeDtypeStruct((N_TOK, LANES), F32),
                   jax.ShapeDtypeStruct((N_GATE, N_TOK), F32)],
        grid=(N_TILES,),
        in_specs=[row(D_MODEL), full(w), full(wt), full(alog), full(dtb), full(alogc), full(dtbc)],
        out_specs=[row(DN_QKV), row(DN_HEADS * DN_DV), row(LANES),
                   pl.BlockSpec((N_GATE, TM), lambda i: (0, i))],
        compiler_params=_params(),
        name="l1_in_proj",
    )(h3, w, wt, alog, dtb, alogc, dtbc)


HALO = SUBLANES
SOLVE_BASE = 16


def _unit_tri_inverse(a, blk16, blk32):
    x = jnp.where(blk16, -a, 0.0)
    xb = x.astype(BF16)
    x2 = _dot(xb, xb)
    yield
    x2b = x2.astype(BF16)
    z = _dot(jnp.concatenate([x2b, xb], axis=0), x2b)
    yield
    x4 = z[:CHUNK]
    s = (x + x2) + z[CHUNK:]
    x4b = x4.astype(BF16)
    z = _dot(jnp.concatenate([x4b, s.astype(BF16)], axis=0), x4b)
    yield
    s = (s + x4) + z[CHUNK:]
    x8 = z[:CHUNK]
    z = _dot(s.astype(BF16), x8.astype(BF16))
    yield
    s = (s + x8) + z
    for off in (jnp.where(blk32 & ~blk16, a, 0.0), jnp.where(blk32, 0.0, a)):
        sb = s.astype(BF16)
        z = _dot(sb, off.astype(BF16))
        yield
        y = off + z
        z = _dot(y.astype(BF16), sb)
        yield
        s = s - (y + z)
    return s


def _dn_kernel(q_ref, k_ref, v_ref, cwq_ref, cwk_ref, cwv_ref, g_ref, gt_ref, s0_ref,
               o_ref, st_ref,
               pad_ref, wq_ref, u_ref, qk_ref, ke_ref, dl_ref, *, seq_len, heads, unroll):
    n = seq_len // CHUNK
    h0 = pl.program_id(1) * heads
    lower, upper = _tri_masks(CHUNK)
    strict_lower = lower & ~upper
    strict_upper = upper & ~lower
    lower_bf = lower.astype(BF16)
    upper_bf = upper.astype(BF16)
    ri = lax.broadcasted_iota(jnp.int32, (CHUNK, CHUNK), 0)
    ci = lax.broadcasted_iota(jnp.int32, (CHUNK, CHUNK), 1)
    blk16 = (ri // SOLVE_BASE) == (ci // SOLVE_BASE)
    blk32 = (ri // (2 * SOLVE_BASE)) == (ci // (2 * SOLVE_BASE))

    zero = jnp.zeros((HALO, DN_DK), F32)
    for t, src in enumerate((q_ref, k_ref, v_ref)):
        for hh in range(heads):
            pad_ref[t, hh, 0:HALO] = zero
            pad_ref[t, hh, HALO + seq_len:2 * HALO + seq_len] = zero
            pad_ref[t, hh, HALO:HALO + seq_len] = src[:, hh * DN_DK:(hh + 1) * DN_DK]
    o_ref[...] = jnp.zeros_like(o_ref)
    st_ref[...] = s0_ref[...]

    def conv(t, cw_ref, r0, hh):
        acc = None
        for j in range(DN_CONV):
            term = (pad_ref[t, hh, pl.ds(r0 + (HALO - DN_CONV // 2 + j), CHUNK)]
                    * cw_ref[j:j + 1, hh * DN_DK:(hh + 1) * DN_DK])
            acc = term if acc is None else acc + term
        return _silu(acc)

    def l2n(x):
        return x * lax.rsqrt(jnp.sum(x * x, axis=-1, keepdims=True) + EPS)

    lane = lax.broadcasted_iota(jnp.int32, (CHUNK, LANES), 1)

    def gate_col(gblk, col):
        return jnp.sum(jnp.where(lane == col, gblk, 0.0), axis=-1, keepdims=True)

    directions = ((lower, strict_lower, CHUNK - 1), (upper, strict_upper, 0))

    gate_row = lax.broadcasted_iota(jnp.int32, (N_GATE, CHUNK), 0)

    def prep_chain(c, hh, d, q, k, v, gblk, gcum, grow):
        mask, strict, tr = directions[d]
        beta = gate_col(gblk, h0 + hh + (2 + d) * DN_HEADS)
        gc = jnp.broadcast_to(gate_col(gcum[d], h0 + hh + d * DN_HEADS), (CHUNK, DN_DK))
        gr = jnp.sum(jnp.where(gate_row == h0 + hh + d * DN_HEADS, grow[d], 0.0), axis=0, keepdims=True)
        k_beta = k * beta
        kq = _dot_nt(jnp.concatenate([k_beta, q], axis=0).astype(BF16), k.astype(BF16))
        yield
        decay = jnp.exp(jnp.where(mask, gc[:, :CHUNK] - gr, -jnp.inf))
        a = jnp.where(strict, kq[:CHUNK] * decay, 0.0)
        qk = kq[CHUNK:] * decay
        eg = jnp.exp(gc)
        rhs = jnp.concatenate([v * beta, k_beta * eg], axis=-1)
        s = yield from _unit_tri_inverse(a, blk16, blk32)
        z = _dot(s.astype(BF16), rhs.astype(BF16))
        yield
        r = rhs + z
        tot = gc[tr:tr + 1]
        idx = (2 * hh + d) * n + c
        u_ref[idx] = r[:, :DN_DV]
        wq_ref[idx] = jnp.concatenate([r[:, DN_DV:], q * eg], axis=0).astype(BF16)
        qk_ref[idx] = qk.astype(BF16)
        ke_ref[idx] = (k * jnp.exp(tot - gc)).astype(BF16)
        dl_ref[idx] = jnp.broadcast_to(jnp.exp(tot), (SUBLANES, DN_DV))

    def prep(cg, carry):
        chains = []
        for uu in range(unroll):
            c = cg * unroll + uu
            r0 = pl.multiple_of(c * CHUNK, CHUNK)
            gblk = g_ref[pl.ds(r0, CHUNK)]
            gtblk = gt_ref[c]
            gcum = (_mask_dot(lower_bf, gblk), _mask_dot(upper_bf, gblk))
            grow = (_dot_mask(gtblk, upper_bf), _dot_mask(gtblk, lower_bf))
            for hh in range(heads):
                q = l2n(conv(0, cwq_ref, r0, hh)) * (DN_DK ** -0.5)
                k = l2n(conv(1, cwk_ref, r0, hh))
                v = conv(2, cwv_ref, r0, hh)
                for d in range(2):
                    chains.append(prep_chain(c, hh, d, q, k, v, gblk, gcum, grow))
        _interleave(chains)
        return carry

    lax.fori_loop(0, n // unroll, prep, 0)

    def scan_chain(c, hh, d):
        cc = c if d == 0 else n - 1 - c
        idx = (2 * hh + d) * n + cc
        s = st_ref[0, d, hh]
        rows = pl.ds(pl.multiple_of(cc * CHUNK, CHUNK), CHUNK)
        cols = slice(hh * DN_DV, (hh + 1) * DN_DV)
        z = _dot(wq_ref[idx], s.astype(BF16))
        yield
        v_new = (u_ref[idx] - z[:CHUNK]).astype(BF16)
        zo = _dot(qk_ref[idx], v_new)
        zs = _dot_tn(ke_ref[idx], v_new)
        yield
        o_ref[rows, cols] += z[CHUNK:] + zo
        st_ref[0, d, hh] = s * dl_ref[idx][0:1] + zs

    def scan(c, carry):
        _interleave([scan_chain(c, hh, d) for hh in range(heads) for d in range(2)])
        return carry

    lax.fori_loop(0, n, scan, 0)


def _deltanet(qkv, conv_w, gates, gates_t, s0, *, seq_len, n_seq, row_off, heads, unroll):
    off = row_off // seq_len
    n = seq_len // CHUNK
    groups = DN_HEADS // heads
    width = heads * DN_DK
    col = lambda t: pl.BlockSpec((seq_len, width), lambda s, p: (s + off, t * groups + p))
    cw = lambda t: pl.BlockSpec((DN_CONV, width), lambda s, p: (0, t * groups + p))
    st_spec = pl.BlockSpec((1, 2, heads, DN_DK, DN_DV), lambda s, p: (s, 0, p, 0, 0))
    slots = 2 * heads * n
    return pl.pallas_call(
        functools.partial(_dn_kernel, seq_len=seq_len, heads=heads, unroll=unroll),
        out_shape=[jax.ShapeDtypeStruct((n_seq * seq_len, DN_HEADS * DN_DV), F32),
                   jax.ShapeDtypeStruct((n_seq, 2, DN_HEADS, DN_DK, DN_DV), F32)],
        grid=(n_seq, groups),
        in_specs=[col(0), col(1), col(2), cw(0), cw(1), cw(2),
                  pl.BlockSpec((seq_len, LANES), lambda s, p: (s + off, 0)),
                  pl.BlockSpec((n, N_GATE, CHUNK), lambda s, p: (s + off, 0, 0)),
                  st_spec],
        out_specs=[pl.BlockSpec((seq_len, width), lambda s, p: (s, p)), st_spec],
        scratch_shapes=[pltpu.VMEM((3, heads, seq_len + 2 * HALO, DN_DK), F32),
                        pltpu.VMEM((slots, 2 * CHUNK, DN_DK), BF16),
                        pltpu.VMEM((slots, CHUNK, DN_DV), F32),
                        pltpu.VMEM((slots, CHUNK, CHUNK), BF16),
                        pltpu.VMEM((slots, CHUNK, DN_DK), BF16),
                        pltpu.VMEM((slots, SUBLANES, DN_DV), F32)],
        compiler_params=_params(),
        name=f"deltanet_{seq_len}",
    )(qkv, qkv, qkv, conv_w, conv_w, conv_w, gates, gates_t, s0)


def _l1_out_kernel(oc_ref, ol_ref, z_ref, on_ref, w_ref, x_ref, mod_ref, n2_ref, rh_ref, rl_ref,
                   x3_ref, h4_ref, route_ref):
    m = mod_ref[0]
    on = on_ref[...]
    parts = []
    for hd in range(DN_HEADS):
        cs = slice(hd * DN_DV, (hd + 1) * DN_DV)
        parts.append((_rms(_pick_rows(oc_ref, ol_ref, cs), on) * _silu(z_ref[:, cs])).astype(BF16))
    y = _dot(jnp.concatenate(parts, axis=-1), w_ref[...])
    x3 = x_ref[...] + m[2:3] * y
    x3_ref[...] = x3
    h4 = _rms(x3, n2_ref[...]) * (1.0 + m[4:5]) + m[3:4]
    h4_ref[...] = h4
    h4_hi = h4.astype(BF16)
    h4_lo = (h4 - h4_hi.astype(F32)).astype(BF16)
    logits = (_dot(h4_hi, rh_ref[...]) + _dot(h4_hi, rl_ref[...])) + _dot(h4_lo, rh_ref[...])
    lane = lax.broadcasted_iota(jnp.int32, logits.shape, 1).astype(F32)
    logits = jnp.where(lane < N_EXPERTS, logits, -jnp.inf)
    m1 = jnp.max(logits, axis=-1, keepdims=True)
    i1 = jnp.min(jnp.where(logits == m1, lane, float(LANES)), axis=-1, keepdims=True)
    rest = jnp.where(lane == i1, -jnp.inf, logits)
    m2 = jnp.max(rest, axis=-1, keepdims=True)
    i2 = jnp.min(jnp.where(rest == m2, lane, float(LANES)), axis=-1, keepdims=True)
    e2 = jnp.exp(m2 - m1)
    g1 = 1.0 / (1.0 + e2)
    g2 = e2 / (1.0 + e2)
    route_ref[...] = jnp.where(lane == 0, i1, jnp.where(lane == 1, i2, jnp.where(
        lane == 2, g1, jnp.where(lane == 3, g2, 0.0))))


def _l1_out(o_dn_c, o_dn_l, z, onorm, w, x2, mod, norm2, r_hi, r_lo):
    row = lambda n: pl.BlockSpec((TM, n), lambda i: (i, 0))
    full = lambda a: pl.BlockSpec(a.shape, lambda i: (0,) * a.ndim)
    return pl.pallas_call(
        _l1_out_kernel,
        out_shape=[jax.ShapeDtypeStruct((N_TOK, D_MODEL), F32),
                   jax.ShapeDtypeStruct((N_TOK, D_MODEL), F32),
                   jax.ShapeDtypeStruct((N_TOK, LANES), F32)],
        grid=(N_TILES,),
        in_specs=[_ctx_rows(D_MODEL), _lat_rows(D_MODEL), row(D_MODEL), full(onorm), full(w),
                  row(D_MODEL), pl.BlockSpec((1, N_MOD, D_MODEL), lambda i: (_mod_index(i), 0, 0)),
                  full(norm2), full(r_hi), full(r_lo)],
        out_specs=[row(D_MODEL), row(D_MODEL), row(LANES)],
        compiler_params=_params(),
        name="l1_out_proj_router",
    )(o_dn_c, o_dn_l, z, onorm, w, x2, mod, norm2, r_hi, r_lo)


ROW_DMA_UNROLL = 8


def _moe_kernel(te_ref, nt_ref, nv_ref, src_ref, dst_ref, h_hbm, w1_ref, w3_ref, w2_ref, y_hbm,
                xs_ref, xb_ref, acc_ref, yb_ref, gsem, ssem):
    i = pl.program_id(0)
    j = pl.program_id(1)
    n_used = nt_ref[0]
    slot = i % 2

    def for_valid_rows(tile, fn):
        nv = nv_ref[tile]
        groups = nv // ROW_DMA_UNROLL

        def group(g, carry):
            for u in range(ROW_DMA_UNROLL):
                fn(g * ROW_DMA_UNROLL + u)
            return carry

        def single(r, carry):
            fn(r)
            return carry

        lax.fori_loop(0, groups, group, 0)
        lax.fori_loop(groups * ROW_DMA_UNROLL, nv, single, 0)

    def gather(tile, sl, start):
        def row(r):
            cp = pltpu.make_async_copy(h_hbm.at[pl.ds(src_ref[tile * MOE_TM + r], 1)],
                                       xs_ref.at[sl, pl.ds(r, 1)], gsem.at[sl])
            cp.start() if start else cp.wait()
        for_valid_rows(tile, row)

    def scatter(tile, sl, start):
        def row(r):
            cp = pltpu.make_async_copy(yb_ref.at[sl, pl.ds(r, 1)],
                                       y_hbm.at[pl.ds(dst_ref[tile * MOE_TM + r], 1)], ssem.at[sl])
            cp.start() if start else cp.wait()
        for_valid_rows(tile, row)

    @pl.when((i < n_used) & (j == 0))
    def _():
        @pl.when(i == 0)
        def _():
            xs_ref[...] = jnp.zeros_like(xs_ref)
            gather(i, slot, True)
        gather(i, slot, False)

        @pl.when(i + 1 < n_used)
        def _():
            gather(i + 1, 1 - slot, True)
        xb_ref[...] = xs_ref[slot].astype(BF16)
        acc_ref[...] = jnp.zeros_like(acc_ref)

    @pl.when(i < n_used)
    def _():
        x = xb_ref[...]
        g = (_silu(_dot(x, w1_ref[0])) * _dot(x, w3_ref[0])).astype(BF16)
        acc_ref[...] += _dot(g, w2_ref[0])

    @pl.when((i < n_used) & (j == pl.num_programs(1) - 1))
    def _():
        @pl.when(i >= 2)
        def _():
            scatter(i - 2, slot, False)
        yb_ref[slot] = acc_ref[...]
        scatter(i, slot, True)

    @pl.when((i == pl.num_programs(0) - 1) & (j == pl.num_programs(1) - 1))
    def _():
        @pl.when(n_used >= 2)
        def _():
            scatter(n_used - 2, n_used % 2, False)
        scatter(n_used - 1, (n_used - 1) % 2, False)


def _moe(tile_expert, n_used, n_valid, src_rows, dst_rows, h4, w1, w3, w2):
    def wmap(i, j, te, nt, nv, src, dst):
        return (te[i], 0, jnp.where(i < nt[0], j, MOE_FF_STEPS - 1))

    def w2map(i, j, te, nt, nv, src, dst):
        return (te[i], jnp.where(i < nt[0], j, MOE_FF_STEPS - 1), 0)

    grid_spec = pltpu.PrefetchScalarGridSpec(
        num_scalar_prefetch=5,
        grid=(MOE_TILES, MOE_FF_STEPS),
        in_specs=[pl.BlockSpec(memory_space=pl.ANY),
                  pl.BlockSpec((1, D_MODEL, MOE_FF), wmap),
                  pl.BlockSpec((1, D_MODEL, MOE_FF), wmap),
                  pl.BlockSpec((1, MOE_FF, D_MODEL), w2map)],
        out_specs=pl.BlockSpec(memory_space=pl.ANY),
        scratch_shapes=[pltpu.VMEM((2, MOE_TM, D_MODEL), F32),
                        pltpu.VMEM((MOE_TM, D_MODEL), BF16),
                        pltpu.VMEM((MOE_TM, D_MODEL), F32),
                        pltpu.VMEM((2, MOE_TM, D_MODEL), F32),
                        pltpu.SemaphoreType.DMA((2,)),
                        pltpu.SemaphoreType.DMA((2,))])
    return pl.pallas_call(
        _moe_kernel,
        out_shape=jax.ShapeDtypeStruct((2 * N_TOK, D_MODEL), F32),
        grid_spec=grid_spec,
        compiler_params=_params(has_side_effects=True),
        name="routed_experts",
    )(tile_expert, n_used, n_valid, src_rows, dst_rows, h4, w1, w3, w2)


def _combine_kernel(y0_ref, y1_ref, route_ref, x_ref, mod_ref, fn_ref, o_ref):
    m = mod_ref[0]
    route = route_ref[...]
    y = route[:, 2:3] * y0_ref[...] + route[:, 3:4] * y1_ref[...]
    o_ref[...] = _rms(x_ref[...] + m[5:6] * y, fn_ref[...])


def _combine(y, route, x3, mod, final_norm):
    row = lambda n: pl.BlockSpec((TM, n), lambda i: (i, 0))
    return pl.pallas_call(
        _combine_kernel,
        out_shape=jax.ShapeDtypeStruct((N_TOK, D_MODEL), F32),
        grid=(N_TILES,),
        in_specs=[row(D_MODEL), pl.BlockSpec((TM, D_MODEL), lambda i: (i + N_TILES, 0)),
                  row(LANES), row(D_MODEL),
                  pl.BlockSpec((1, N_MOD, D_MODEL), lambda i: (_mod_index(i), 0, 0)),
                  pl.BlockSpec((1, D_MODEL), lambda i: (0, 0))],
        out_specs=row(D_MODEL),
        compiler_params=_params(),
        name="combine_final_norm",
    )(y, y, route, x3, mod, final_norm)


def _rope_tables():
    rows = LAT_LEN // GRID_W
    row = jnp.repeat(jnp.arange(rows), GRID_W).astype(F32)
    col = (jnp.arange(rows * GRID_W) % GRID_W).astype(F32)
    half = SWA_HD // 2
    inv_freq = ROPE_BASE ** (-jnp.arange(0, half, 2, dtype=F32) / half)
    ang_r = row[:, None] * inv_freq
    ang_c = col[:, None] * inv_freq
    cos = jnp.concatenate([jnp.cos(ang_r)] * 2 + [jnp.cos(ang_c)] * 2, axis=-1)
    sin = jnp.concatenate([-jnp.sin(ang_r), jnp.sin(ang_r), -jnp.sin(ang_c), jnp.sin(ang_c)], axis=-1)
    cos = jnp.tile(cos, (1, SWA_HEADS))
    sin = jnp.tile(sin, (1, SWA_HEADS))
    cos = jnp.concatenate([cos, jnp.ones((TM, cos.shape[1]), F32)], axis=0)
    sin = jnp.concatenate([sin, jnp.zeros((TM, sin.shape[1]), F32)], axis=0)
    return cos, sin


def _dispatch(route):
    e = jnp.concatenate([route[:, 0], route[:, 1]]).astype(jnp.int32)
    onehot = (e[:, None] == jnp.arange(N_EXPERTS)[None, :]).astype(jnp.int32)
    counts = jnp.sum(onehot, axis=0)
    rank = jnp.sum((jnp.cumsum(onehot, axis=0) - onehot) * onehot, axis=1)
    padded = ((counts + MOE_TM - 1) // MOE_TM) * MOE_TM
    ends = jnp.cumsum(padded)
    starts = ends - padded
    pos = starts[e] + rank
    dst = jnp.zeros((MOE_ROWS,), jnp.int32).at[pos].set(jnp.arange(2 * N_TOK, dtype=jnp.int32))
    src = dst % N_TOK
    n_used = (ends[-1] // MOE_TM).astype(jnp.int32)
    tile_start = jnp.arange(MOE_TILES, dtype=jnp.int32) * MOE_TM
    tile_expert = jnp.minimum(jnp.sum(tile_start[:, None] >= ends[None, :], axis=1), N_EXPERTS - 1)
    n_valid = jnp.clip((starts + counts)[tile_expert] - tile_start, 0, MOE_TM).astype(jnp.int32)
    last = jnp.take(tile_expert, jnp.maximum(n_used - 1, 0))
    tile_expert = jnp.where(jnp.arange(MOE_TILES) < n_used, tile_expert, last).astype(jnp.int32)
    return tile_expert, n_used.reshape(1), n_valid, src, dst


def kernel(x_prompt, x_sample, c, state_gla_l0, cache_k_l0, cache_v_l0, state_dn_l1, c_ctx, final_norm, ada_w_l0, ada_b_l0, norm1_l0, norm2_l0, in_w_l0, gla_w2_l0, gla_b_l0, gla_onorm_l0, sink_l0, out_w_l0, ffn_w1_l0, ffn_w3_l0, ffn_w2_l0, ada_w_l1, ada_b_l1, norm1_l1, norm2_l1, in_w_l1, conv_w_l1, a_log_l1, dt_bias_l1, dn_onorm_l1, out_w_l1, router_l1, moe_w1_l1, moe_w3_l1, moe_w2_l1):
    D = D_MODEL
    x_ctx = x_prompt.reshape(N_CTX_TOK, D)
    x_lat = x_sample.reshape(N_LAT_TOK, D)
    row2 = lambda v: v.reshape(1, -1)

    cond = jnp.concatenate([c_ctx[None], c, jnp.zeros((2 * SUBLANES - 1 - N_LAT_SEQ, D), F32)], axis=0)
    mod0 = _modulation(cond, ada_w_l0, ada_b_l0).reshape(-1, N_MOD, D)
    mod1 = _modulation(cond, ada_w_l1, ada_b_l1).reshape(-1, N_MOD, D)

    gq_w, gk_w, gv_w, gr_w, glrf_w, glrb_w, sq_w, sk_w, sv_w = jnp.split(
        in_w_l0, [256, 512, 1024, 1536, 1552, 1568, 2080, 2208], axis=1)
    w0 = jnp.concatenate([gq_w, gk_w, gv_w, gr_w, sq_w, sk_w, sv_w, glrf_w, glrb_w,
                          jnp.zeros((D, L0_END - L0_GLR - 2 * GLA_RANK), F32)], axis=1).astype(BF16)
    w2f = jnp.zeros((LANES, GLA_HEADS * GLA_DK), F32).at[:GLA_RANK].set(gla_w2_l0[0]).astype(BF16)
    w2b = jnp.zeros((LANES, GLA_HEADS * GLA_DK), F32).at[GLA_RANK:2 * GLA_RANK].set(gla_w2_l0[1]).astype(BF16)
    cos, sin = _rope_tables()
    gq, gk, gv, gr, laf, lab, sq, sk, sv = _l0_in(x_ctx, x_lat, mod0, row2(norm1_l0), w0, w2f, w2b,
                                                  gla_b_l0, cos, sin)

    onorm0 = row2(gla_onorm_l0)
    zero_state = jnp.zeros((N_CTX_SEQ, 2, GLA_HEADS, GLA_DV, GLA_DK), F32)
    o_gla_c, st_gla = _gla(gq, gk, gv, gr, laf, lab, zero_state, onorm0,
                           seq_len=CTX_LEN, n_seq=N_CTX_SEQ, row_off=0)
    o_gla_l, _ = _gla(gq, gk, gv, gr, laf, lab, jnp.swapaxes(state_gla_l0, -1, -2), onorm0,
                      seq_len=LAT_LEN, n_seq=N_LAT_SEQ, row_off=N_CTX_TOK)
    o_swa_c = _swa_ctx(sink_l0, sq, sk, sv)
    o_swa_l = _swa_lat(sink_l0, sq, sk, sv,
                       cache_k_l0.reshape(N_LAT_SEQ, -1, SWA_KV_HEADS * SWA_HD),
                       cache_v_l0.reshape(N_LAT_SEQ, -1, SWA_KV_HEADS * SWA_HD))
    x1, h2 = _l0_out(o_gla_c, o_gla_l, o_swa_c, o_swa_l, out_w_l0.astype(BF16), x_ctx, x_lat, mod0,
                     row2(norm2_l0))

    nff = D_FF_DENSE // FF_CHUNK
    w1 = ffn_w1_l0.astype(BF16).reshape(D, nff, FF_CHUNK).transpose(1, 0, 2)
    w3 = ffn_w3_l0.astype(BF16).reshape(D, nff, FF_CHUNK).transpose(1, 0, 2)
    w2 = ffn_w2_l0.astype(BF16).reshape(nff, FF_CHUNK, D)
    x2, h3 = _ffn(h2, x1, w1, w3, w2, mod0, mod1, row2(norm1_l1))

    w_ab = in_w_l1[:, L1_AB:]
    wl1 = jnp.concatenate([in_w_l1, jnp.zeros((D, L1_END - in_w_l1.shape[1]), F32)], axis=1).astype(BF16)
    zeros_g = jnp.zeros((2 * DN_HEADS,), F32)
    alog = jnp.concatenate([a_log_l1.reshape(-1), zeros_g])
    dtb = jnp.concatenate([dt_bias_l1.reshape(-1), zeros_g])
    pad_lane = lambda v: jnp.concatenate([v, jnp.zeros((LANES - N_GATE,), F32)]).reshape(1, LANES)
    qkv, z, gates, gates_t = _l1_in(h3, wl1, w_ab.T.astype(BF16), pad_lane(alog), pad_lane(dtb),
                                    alog.reshape(N_GATE, 1), dtb.reshape(N_GATE, 1))
    gates_t = gates_t.reshape(N_GATE, N_TOK // CHUNK, CHUNK).transpose(1, 0, 2)
    zero_dn = jnp.zeros((N_CTX_SEQ, 2, DN_HEADS, DN_DK, DN_DV), F32)
    o_dn_c, st_dn = _deltanet(qkv, conv_w_l1, gates, gates_t, zero_dn,
                              seq_len=CTX_LEN, n_seq=N_CTX_SEQ, row_off=0, heads=4, unroll=2)
    o_dn_l, _ = _deltanet(qkv, conv_w_l1, gates, gates_t, state_dn_l1,
                          seq_len=LAT_LEN, n_seq=N_LAT_SEQ, row_off=N_CTX_TOK, heads=2, unroll=4)
    router = jnp.concatenate([router_l1, jnp.zeros((D, LANES - N_EXPERTS), F32)], axis=1)
    r_hi = router.astype(BF16)
    r_lo = (router - r_hi.astype(F32)).astype(BF16)
    x3, h4, route = _l1_out(o_dn_c, o_dn_l, z, row2(dn_onorm_l1), out_w_l1.astype(BF16), x2, mod1,
                            row2(norm2_l1), r_hi, r_lo)

    tile_expert, n_used, n_valid, src, dst = _dispatch(route)
    y = _moe(tile_expert, n_used, n_valid, src, dst, h4,
             moe_w1_l1.astype(BF16), moe_w3_l1.astype(BF16), moe_w2_l1.astype(BF16))
    out = _combine(y, route, x3, mod1, row2(final_norm))

    y_prompt = out[:N_CTX_TOK].reshape(N_CTX_SEQ, CTX_LEN, D)
    y_sample = out[N_CTX_TOK:].reshape(N_LAT_SEQ, LAT_LEN, D)
    new_state_gla = jnp.swapaxes(st_gla, -1, -2)
    new_k = sk[:N_CTX_TOK].reshape(N_CTX_SEQ, CTX_LEN, SWA_KV_HEADS, SWA_HD)
    new_v = sv[:N_CTX_TOK].reshape(N_CTX_SEQ, CTX_LEN, SWA_KV_HEADS, SWA_HD)
    return (y_prompt, y_sample, new_state_gla, new_k, new_v, st_dn)
```

```python
import functools
import math

import jax
import jax.numpy as jnp
from jax import lax
from jax.experimental import pallas as pl
from jax.experimental.pallas import tpu as pltpu

F32 = jnp.float32
BF16 = jnp.bfloat16

D_MODEL = 1024
N_CTX_SEQ = 16
CTX_LEN = 256
N_LAT_SEQ = 8
LAT_LEN = 2048
N_CTX_TOK = N_CTX_SEQ * CTX_LEN
N_LAT_TOK = N_LAT_SEQ * LAT_LEN
N_TOK = N_CTX_TOK + N_LAT_TOK
GRID_W = 64
EPS = 1e-6
N_MOD = 6

GLA_HEADS = 4
GLA_DK = 64
GLA_DV = 128
GLA_RANK = 16
GLA_TAU = 16.0
CHUNK = 64
GLA_UNROLL = 4

SWA_HEADS = 8
SWA_KV_HEADS = 2
SWA_GROUP = SWA_HEADS // SWA_KV_HEADS
SWA_HD = 64
SWA_BLOCK = 128
ROPE_BASE = 10000.0

DN_HEADS = 8
DN_DK = 128
DN_DV = 128
DN_CONV = 5
DN_QKV = 3 * DN_HEADS * DN_DK

D_FF_DENSE = 2816
N_EXPERTS = 8
D_FF_EXPERT = 3584

LANES = 128
SUBLANES = 8
MXU_DIM = 256
VMEM_LIMIT = 56 * 1024 * 1024

TM = 512
N_TILES = N_TOK // TM
CTX_TILES = N_CTX_TOK // TM
LAT_TILES_PER_SEQ = LAT_LEN // TM
FF_CHUNK = MXU_DIM
MOE_TM = 512
MOE_FF = 1792
MOE_ROWS = 2 * N_TOK + N_EXPERTS * MOE_TM
MOE_TILES = MOE_ROWS // MOE_TM
MOE_FF_STEPS = D_FF_EXPERT // MOE_FF


def _params(**kw):
    return pltpu.CompilerParams(vmem_limit_bytes=VMEM_LIMIT, **kw)


def _sigmoid(x):
    return 1.0 / (1.0 + jnp.exp(-x))


def _silu(x):
    return x * _sigmoid(x)


def _softplus(x):
    return jnp.maximum(x, 0.0) + jnp.log(1.0 + jnp.exp(-jnp.abs(x)))


def _rms(x, g):
    return x * lax.rsqrt(jnp.mean(x * x, axis=-1, keepdims=True) + EPS) * g


def _dot(a, b):
    return jnp.dot(a, b, preferred_element_type=F32)


def _dot_nt(a, b):
    return lax.dot_general(a, b, (((1,), (1,)), ((), ())), preferred_element_type=F32)


def _dot_tn(a, b):
    return lax.dot_general(a, b, (((0,), (0,)), ((), ())), preferred_element_type=F32)


def _split3(x):
    hi = x.astype(BF16)
    r = x - hi.astype(F32)
    mid = r.astype(BF16)
    lo = (r - mid.astype(F32)).astype(BF16)
    return hi, mid, lo


def _mask_dot(mask_bf, x):
    hi, mid, lo = _split3(x)
    return (_dot(mask_bf, hi) + _dot(mask_bf, mid)) + _dot(mask_bf, lo)


def _dot_mask(x, mask_bf):
    hi, mid, lo = _split3(x)
    return (_dot(hi, mask_bf) + _dot(mid, mask_bf)) + _dot(lo, mask_bf)


def _interleave(chains):
    results = [None] * len(chains)
    active = list(enumerate(chains))
    while active:
        still = []
        for i, chain in active:
            try:
                next(chain)
                still.append((i, chain))
            except StopIteration as stop:
                results[i] = stop.value
        active = still
    return results


def _tri_masks(n):
    r = lax.broadcasted_iota(jnp.int32, (n, n), 0)
    c = lax.broadcasted_iota(jnp.int32, (n, n), 1)
    return c <= r, c >= r


def _ctx_rows(w):
    return pl.BlockSpec((TM, w), lambda i: (jnp.minimum(i, CTX_TILES - 1), 0))


def _lat_rows(w):
    return pl.BlockSpec((TM, w), lambda i: (jnp.maximum(i - CTX_TILES, 0), 0))


def _pick_rows(ctx_ref, lat_ref, cols=slice(None)):
    return jnp.where(pl.program_id(0) < CTX_TILES, ctx_ref[:, cols], lat_ref[:, cols])


ROW_TILES = D_MODEL // LANES


def _store_row_tiles(ref, x, lead=()):
    rows = x.shape[0]
    for s in range(ROW_TILES):
        ref[lead + (pl.ds(s, rows, stride=ROW_TILES), slice(None))] = x[:, s * LANES:(s + 1) * LANES]


def _load_row_tiles(ref, rows, lead=()):
    return jnp.concatenate([ref[lead + (pl.ds(s, rows, stride=ROW_TILES), slice(None))]
                            for s in range(ROW_TILES)], axis=-1)


def _mod_index(i):
    return jnp.where(i < CTX_TILES, 0, 1 + (i - CTX_TILES) // LAT_TILES_PER_SEQ)


def _mod_kernel(c_ref, w_ref, b_ref, o_ref):
    s = _silu(c_ref[...]).astype(BF16)
    o_ref[...] = _dot(s, w_ref[...].astype(BF16)) + b_ref[...]


def _modulation(cond, w, b):
    m, d = cond.shape
    n = w.shape[1]
    tn = n // 4
    return pl.pallas_call(
        _mod_kernel,
        out_shape=jax.ShapeDtypeStruct((m, n), F32),
        grid=(n // tn,),
        in_specs=[pl.BlockSpec((m, d), lambda j: (0, 0)),
                  pl.BlockSpec((d, tn), lambda j: (0, j)),
                  pl.BlockSpec((1, tn), lambda j: (0, j))],
        out_specs=pl.BlockSpec((m, tn), lambda j: (0, j)),
        compiler_params=_params(),
        name="modulation",
    )(cond, w, b.reshape(1, n))


L0_GQ, L0_GK, L0_GV, L0_GR, L0_SQ, L0_SK, L0_SV, L0_GLR, L0_END = (
    0, 256, 512, 1024, 1536, 2048, 2176, 2304, 2432)


def _rope(x, cos, sin):
    n = x.shape[-1]
    lane = lax.broadcasted_iota(jnp.int32, x.shape, 1)
    first = (lane % 32) < 16
    partner = jnp.where(first, pltpu.roll(x, n - 16, 1), pltpu.roll(x, 16, 1))
    return x * cos + partner * sin


def _log_sigmoid(x):
    return jnp.minimum(x, 0.0) - jnp.log(1.0 + jnp.exp(-jnp.abs(x)))


def _l0_in_kernel(xc_ref, xl_ref, mod_ref, n1_ref, w_ref, w2f_ref, w2b_ref, gb_ref, cos_ref, sin_ref,
                  gq_ref, gk_ref, gv_ref, gr_ref, laf_ref, lab_ref, sq_ref, sk_ref, sv_ref):
    m = mod_ref[0]
    h = _rms(_pick_rows(xc_ref, xl_ref), n1_ref[...]) * (1.0 + m[1:2]) + m[0:1]
    hb = h.astype(BF16)

    def proj(a, b):
        return _dot(hb, w_ref[:, a:b])

    gq_ref[...] = proj(L0_GQ, L0_GK) * (GLA_DK ** -0.5)
    gk_ref[...] = proj(L0_GK, L0_GV)
    gv_ref[...] = proj(L0_GV, L0_GR)
    gr_ref[...] = proj(L0_GR, L0_SQ)
    cos = cos_ref[...]
    sin = sin_ref[...]
    sq_ref[...] = _rope(proj(L0_SQ, L0_SK), cos, sin)
    nk = L0_SV - L0_SK
    sk_ref[...] = _rope(proj(L0_SK, L0_SV), cos[:, :nk], sin[:, :nk])
    sv_ref[...] = proj(L0_SV, L0_GLR)
    glr = proj(L0_GLR, L0_END).astype(BF16)
    gb = gb_ref[...]
    laf_ref[...] = _log_sigmoid(_dot(glr, w2f_ref[...]) + gb[0:1]) * (1.0 / GLA_TAU)
    lab_ref[...] = _log_sigmoid(_dot(glr, w2b_ref[...]) + gb[1:2]) * (1.0 / GLA_TAU)


def _rope_index(i):
    return jnp.where(i < CTX_TILES, LAT_TILES_PER_SEQ, (i - CTX_TILES) % LAT_TILES_PER_SEQ)


def _l0_in(x_ctx, x_lat, mod, norm1, w, w2f, w2b, gb, cos, sin):
    row = lambda n: pl.BlockSpec((TM, n), lambda i: (i, 0))
    full = lambda a: pl.BlockSpec(a.shape, lambda i: (0,) * a.ndim)
    widths = (256, 256, 512, 512, 256, 256, 512, 128, 128)
    return pl.pallas_call(
        _l0_in_kernel,
        out_shape=[jax.ShapeDtypeStruct((N_TOK, n), F32) for n in widths],
        grid=(N_TILES,),
        in_specs=[_ctx_rows(D_MODEL), _lat_rows(D_MODEL),
                  pl.BlockSpec((1, N_MOD, D_MODEL), lambda i: (_mod_index(i), 0, 0)),
                  full(norm1), full(w), full(w2f), full(w2b), full(gb),
                  pl.BlockSpec((TM, 512), lambda i: (_rope_index(i), 0)),
                  pl.BlockSpec((TM, 512), lambda i: (_rope_index(i), 0))],
        out_specs=[row(n) for n in widths],
        compiler_params=_params(),
        name="l0_in_proj",
    )(x_ctx, x_lat, mod, norm1, w, w2f, w2b, gb, cos, sin)


def _gla_kernel(q_ref, k_ref, v_ref, r_ref, laf_ref, lab_ref, s0_ref, on_ref,
                o_ref, st_ref, oacc_ref, *, seq_len):
    n = seq_len // CHUNK
    lower, upper = _tri_masks(CHUNK)
    lower_bf = lower.astype(BF16)
    upper_bf = upper.astype(BF16)
    oacc_ref[...] = jnp.zeros_like(oacc_ref)
    st_ref[...] = s0_ref[...]

    def chain(d, chunks, la_ref, mask_bf, mask, tot_row):
        rows = [pl.ds(pl.multiple_of(c * CHUNK, CHUNK), CHUNK) for c in chunks]
        cums = [_mask_dot(mask_bf, la_ref[r, :]) for r in rows]
        yield
        parts = []
        for r, b2 in zip(rows, cums):
            for hh in range(2):
                ck = slice(hh * GLA_DK, (hh + 1) * GLA_DK)
                cv = slice(hh * GLA_DV, (hh + 1) * GLA_DV)
                b = b2[:, ck]
                tot = b[tot_row:tot_row + 1]
                q = q_ref[r, ck]
                k = k_ref[r, ck]
                q_dec = (q * jnp.exp(b)).astype(BF16)
                k_inv = (k * jnp.exp(-b)).astype(BF16)
                k_end = (k * jnp.exp(tot - b)).astype(BF16)
                vb = v_ref[r, cv].astype(BF16)
                parts.append((r, hh, cv, jnp.exp(tot), vb, q_dec, _dot_nt(q_dec, k_inv), _dot_tn(vb, k_end)))
        yield
        states = [st_ref[0, d, hh] for hh in range(2)]
        outs = []
        for r, hh, cv, decay, vb, q_dec, attn, kv in parts:
            o_intra = _dot(jnp.where(mask, attn, 0.0).astype(BF16), vb)
            o_inter = _dot_nt(q_dec, states[hh].astype(BF16))
            states[hh] = states[hh] * decay + kv
            outs.append((r, cv, o_intra, o_inter))
        yield
        for r, cv, o_intra, o_inter in outs:
            oacc_ref[r, cv] += o_intra + o_inter
        for hh in range(2):
            st_ref[0, d, hh] = states[hh]

    unroll = min(GLA_UNROLL, n)

    def body(g, carry):
        fwd = [g * unroll + t for t in range(unroll)]
        bwd = [n - 1 - c for c in fwd]
        _interleave([chain(0, fwd, laf_ref, lower_bf, lower, CHUNK - 1),
                     chain(1, bwd, lab_ref, upper_bf, upper, 0)])
        return carry

    lax.fori_loop(0, n // unroll, body, 0)
    for hh in range(2):
        cv = slice(hh * GLA_DV, (hh + 1) * GLA_DV)
        o_ref[:, cv] = _rms(oacc_ref[:, cv], on_ref[...]) * _silu(r_ref[:, cv])


def _gla(gq, gk, gv, gr, laf, lab, s0t, onorm, *, seq_len, n_seq, row_off):
    off = row_off // seq_len
    blk = lambda w: pl.BlockSpec((seq_len, w), lambda s, p: (s + off, p))
    st_spec = pl.BlockSpec((1, 2, 2, GLA_DV, GLA_DK), lambda s, p: (s, 0, p, 0, 0))
    return pl.pallas_call(
        functools.partial(_gla_kernel, seq_len=seq_len),
        out_shape=[jax.ShapeDtypeStruct((n_seq * seq_len, GLA_HEADS * GLA_DV), F32),
                   jax.ShapeDtypeStruct((n_seq, 2, GLA_HEADS, GLA_DV, GLA_DK), F32)],
        grid=(n_seq, GLA_HEADS // 2),
        in_specs=[blk(2 * GLA_DK), blk(2 * GLA_DK), blk(2 * GLA_DV), blk(2 * GLA_DV),
                  blk(2 * GLA_DK), blk(2 * GLA_DK), st_spec,
                  pl.BlockSpec((1, GLA_DV), lambda s, p: (0, 0))],
        out_specs=[pl.BlockSpec((seq_len, 2 * GLA_DV), lambda s, p: (s, p)), st_spec],
        scratch_shapes=[pltpu.VMEM((seq_len, 2 * GLA_DV), F32)],
        compiler_params=_params(),
        name=f"gla_{seq_len}",
    )(gq, gk, gv, gr, laf, lab, s0t, onorm)


def _attend(q, kcat, vcat, mask, sink_ref):
    def head(h):
        j = h // SWA_GROUP
        cj = slice(j * SWA_HD, (j + 1) * SWA_HD)
        s = _dot_nt(q[:, h * SWA_HD:(h + 1) * SWA_HD].astype(BF16), kcat[:, cj])
        yield
        if mask is not None:
            s = jnp.where(mask, s, -jnp.inf)
        sink = sink_ref[h]
        m = jnp.maximum(jnp.max(s, axis=-1, keepdims=True), sink)
        p = jnp.exp(s - m)
        denom = jnp.sum(p, axis=-1, keepdims=True) + jnp.exp(sink - m)
        o = _dot(p.astype(BF16), vcat[:, cj])
        yield
        return o / denom

    return jnp.concatenate(_interleave([head(h) for h in range(SWA_HEADS)]), axis=-1)


def _swa_ctx_kernel(sink_ref, q_ref, k_ref, v_ref, o_ref):
    o_ref[...] = _attend(q_ref[...] * (SWA_HD ** -0.5), k_ref[...].astype(BF16),
                         v_ref[...].astype(BF16), None, sink_ref)


def _swa_ctx(sink, sq, sk, sv):
    blk = lambda w: pl.BlockSpec((CTX_LEN, w), lambda b: (b, 0))
    return pl.pallas_call(
        _swa_ctx_kernel,
        out_shape=jax.ShapeDtypeStruct((N_CTX_TOK, SWA_HEADS * SWA_HD), F32),
        grid=(N_CTX_SEQ,),
        in_specs=[pl.BlockSpec(memory_space=pltpu.SMEM), blk(512), blk(128), blk(128)],
        out_specs=blk(512),
        compiler_params=_params(),
        name="swa_ctx",
    )(sink, sq, sk, sv)


def _swa_lat_kernel(sink_ref, q_ref, kc_ref, vc_ref, kp_ref, kn_ref, k_ref, vp_ref, vn_ref, v_ref,
                    o_ref):
    i = pl.program_id(1)
    nb = pl.num_programs(1)
    kcat = jnp.concatenate([kc_ref[0], kp_ref[...], k_ref[...], kn_ref[...]], axis=0).astype(BF16)
    vcat = jnp.concatenate([vc_ref[0], vp_ref[...], v_ref[...], vn_ref[...]], axis=0).astype(BF16)
    n_ctx = kc_ref.shape[1]
    n = n_ctx + 3 * SWA_BLOCK
    r = lax.broadcasted_iota(jnp.int32, (SWA_BLOCK, n), 0)
    c = lax.broadcasted_iota(jnp.int32, (SWA_BLOCK, n), 1) - n_ctx
    key_pos = (i - 1) * SWA_BLOCK + c
    dist = SWA_BLOCK + r - c
    mask = (c < 0) | ((jnp.abs(dist) <= SWA_BLOCK) & (key_pos >= 0) & (key_pos < nb * SWA_BLOCK))
    o_ref[...] = _attend(q_ref[...] * (SWA_HD ** -0.5), kcat, vcat, mask, sink_ref)


def _swa_lat(sink, sq, sk, sv, cache_k, cache_v):
    nb = LAT_LEN // SWA_BLOCK
    off = N_CTX_TOK // SWA_BLOCK
    cur = lambda b, i: (off + b * nb + i, 0)
    prev = lambda b, i: (off + b * nb + jnp.maximum(i - 1, 0), 0)
    nxt = lambda b, i: (off + b * nb + jnp.minimum(i + 1, nb - 1), 0)
    kv = lambda f: pl.BlockSpec((SWA_BLOCK, 128), f)
    cache = pl.BlockSpec((1, cache_k.shape[1], 128), lambda b, i: (b, 0, 0))
    return pl.pallas_call(
        _swa_lat_kernel,
        out_shape=jax.ShapeDtypeStruct((N_LAT_TOK, SWA_HEADS * SWA_HD), F32),
        grid=(N_LAT_SEQ, nb),
        in_specs=[pl.BlockSpec(memory_space=pltpu.SMEM),
                  pl.BlockSpec((SWA_BLOCK, 512), cur), cache, cache,
                  kv(prev), kv(nxt), kv(cur), kv(prev), kv(nxt), kv(cur)],
        out_specs=pl.BlockSpec((SWA_BLOCK, 512), lambda b, i: (b * nb + i, 0)),
        compiler_params=_params(),
        name="swa_lat",
    )(sink, sq, cache_k, cache_v, sk, sk, sk, sv, sv, sv)


def _l0_out_kernel(ogc_ref, ogl_ref, osc_ref, osl_ref, w_ref, xc_ref, xl_ref, mod_ref, n2_ref,
                   x1_ref, h2_ref):
    m = mod_ref[0]
    half = ogc_ref.shape[1]
    y = (_dot(_pick_rows(ogc_ref, ogl_ref).astype(BF16), w_ref[:half])
         + _dot(_pick_rows(osc_ref, osl_ref).astype(BF16), w_ref[half:]))
    x1 = _pick_rows(xc_ref, xl_ref) + m[2:3] * y
    x1_ref[...] = x1
    h2_ref[...] = (_rms(x1, n2_ref[...]) * (1.0 + m[4:5]) + m[3:4]).astype(BF16)


def _l0_out(o_gla_c, o_gla_l, o_swa_c, o_swa_l, w, x_ctx, x_lat, mod, norm2):
    row = lambda n: pl.BlockSpec((TM, n), lambda i: (i, 0))
    full = lambda a: pl.BlockSpec(a.shape, lambda i: (0,) * a.ndim)
    return pl.pallas_call(
        _l0_out_kernel,
        out_shape=[jax.ShapeDtypeStruct((N_TOK, D_MODEL), F32),
                   jax.ShapeDtypeStruct((N_TOK, D_MODEL), BF16)],
        grid=(N_TILES,),
        in_specs=[_ctx_rows(512), _lat_rows(512), _ctx_rows(512), _lat_rows(512), full(w),
                  _ctx_rows(D_MODEL), _lat_rows(D_MODEL),
                  pl.BlockSpec((1, N_MOD, D_MODEL), lambda i: (_mod_index(i), 0, 0)), full(norm2)],
        out_specs=[row(D_MODEL), row(D_MODEL)],
        compiler_params=_params(),
        name="l0_out_proj",
    )(o_gla_c, o_gla_l, o_swa_c, o_swa_l, w, x_ctx, x_lat, mod, norm2)


def _ffn_kernel(h_ref, x_ref, w1_ref, w3_ref, w2_ref, mod0_ref, mod1_ref, n1_ref, x2_ref, h3_ref):
    h = h_ref[...]
    acc = jnp.zeros((h.shape[0], D_MODEL), F32)
    for j in range(w1_ref.shape[1] // FF_CHUNK):
        cols = slice(j * FF_CHUNK, (j + 1) * FF_CHUNK)
        g = (_silu(_dot(h, w1_ref[:, cols])) * _dot(h, w3_ref[:, cols])).astype(BF16)
        acc = acc + _dot(g, w2_ref[cols, :])
    m0 = mod0_ref[0]
    m1 = mod1_ref[0]
    x2 = x_ref[...] + m0[5:6] * acc
    x2_ref[...] = x2
    h3_ref[...] = (_rms(x2, n1_ref[...]) * (1.0 + m1[1:2]) + m1[0:1]).astype(BF16)


def _ffn(h2, x1, w1, w3, w2, mod0, mod1, norm1_next):
    row = lambda n: pl.BlockSpec((TM, n), lambda i: (i, 0))
    full = lambda a: pl.BlockSpec(a.shape, lambda i: (0,) * a.ndim)
    mod = pl.BlockSpec((1, N_MOD, D_MODEL), lambda i: (_mod_index(i), 0, 0))
    return pl.pallas_call(
        _ffn_kernel,
        out_shape=[jax.ShapeDtypeStruct((N_TOK, D_MODEL), F32),
                   jax.ShapeDtypeStruct((N_TOK, D_MODEL), BF16)],
        grid=(N_TILES,),
        in_specs=[row(D_MODEL), row(D_MODEL), full(w1), full(w3), full(w2), mod, mod,
                  full(norm1_next)],
        out_specs=[row(D_MODEL), row(D_MODEL)],
        compiler_params=_params(),
        name="dense_ffn",
    )(h2, x1, w1, w3, w2, mod0, mod1, norm1_next)


L1_Z = DN_QKV
L1_AB = DN_QKV + DN_HEADS * DN_DV
L1_END = L1_AB + LANES
N_GATE = 4 * DN_HEADS


def _dn_gates(ab, neg_a, dt_bias, is_decay):
    return jnp.where(is_decay, neg_a * _softplus(ab + dt_bias), _sigmoid(ab))


def _l1_in_kernel(h_ref, w_ref, wt_ref, alog_ref, dtb_ref, alogc_ref, dtbc_ref,
                  qkv_ref, z_ref, g_ref, gt_ref):
    h = h_ref[...]
    qkv_ref[...] = _dot(h, w_ref[:, :L1_Z])
    z_ref[...] = _dot(h, w_ref[:, L1_Z:L1_AB])
    ab = _dot(h, w_ref[:, L1_AB:L1_END])
    lane = lax.broadcasted_iota(jnp.int32, ab.shape, 1)
    g_ref[...] = _dn_gates(ab, -jnp.exp(alog_ref[...]), dtb_ref[...], lane < 2 * DN_HEADS)
    abt = _dot_nt(wt_ref[...], h)
    row = lax.broadcasted_iota(jnp.int32, abt.shape, 0)
    gt_ref[...] = _dn_gates(abt, -jnp.exp(alogc_ref[...]), dtbc_ref[...], row < 2 * DN_HEADS)


def _l1_in(h3, w, wt, alog, dtb, alogc, dtbc):
    row = lambda n: pl.BlockSpec((TM, n), lambda i: (i, 0))
    full = lambda a: pl.BlockSpec(a.shape, lambda i: (0,) * a.ndim)
    return pl.pallas_call(
        _l1_in_kernel,
        out_shape=[jax.ShapeDtypeStruct((N_TOK, DN_QKV), F32),
                   jax.ShapeDtypeStruct((N_TOK, DN_HEADS * DN_DV), F32),
                   jax.ShapeDtypeStruct((N_TOK, LANES), F32),
                   jax.ShapeDtypeStruct((N_GATE, N_TOK), F32)],
        grid=(N_TILES,),
        in_specs=[row(D_MODEL), full(w), full(wt), full(alog), full(dtb), full(alogc), full(dtbc)],
        out_specs=[row(DN_QKV), row(DN_HEADS * DN_DV), row(LANES),
                   pl.BlockSpec((N_GATE, TM), lambda i: (0, i))],
        compiler_params=_params(),
        name="l1_in_proj",
    )(h3, w, wt, alog, dtb, alogc, dtbc)


HALO = SUBLANES
SOLVE_BASE = 16


def _unit_tri_inverse(a, blk16, blk32):
    x = jnp.where(blk16, -a, 0.0)
    xb = x.astype(BF16)
    x2 = _dot(xb, xb)
    yield
    x2b = x2.astype(BF16)
    z = _dot(jnp.concatenate([x2b, xb], axis=0), x2b)
    yield
    x4 = z[:CHUNK]
    s = (x + x2) + z[CHUNK:]
    x4b = x4.astype(BF16)
    z = _dot(jnp.concatenate([x4b, s.astype(BF16)], axis=0), x4b)
    yield
    s = (s + x4) + z[CHUNK:]
    x8 = z[:CHUNK]
    z = _dot(s.astype(BF16), x8.astype(BF16))
    yield
    s = (s + x8) + z
    for off in (jnp.where(blk32 & ~blk16, a, 0.0), jnp.where(blk32, 0.0, a)):
        sb = s.astype(BF16)
        z = _dot(sb, off.astype(BF16))
        yield
        y = off + z
        z = _dot(y.astype(BF16), sb)
        yield
        s = s - (y + z)
    return s


def _dn_kernel(q_ref, k_ref, v_ref, cwq_ref, cwk_ref, cwv_ref, g_ref, gt_ref, s0_ref,
               o_ref, st_ref,
               pad_ref, wq_ref, u_ref, qk_ref, ke_ref, dl_ref, *, seq_len, heads, unroll):
    n = seq_len // CHUNK
    h0 = pl.program_id(1) * heads
    lower, upper = _tri_masks(CHUNK)
    strict_lower = lower & ~upper
    strict_upper = upper & ~lower
    lower_bf = lower.astype(BF16)
    upper_bf = upper.astype(BF16)
    ri = lax.broadcasted_iota(jnp.int32, (CHUNK, CHUNK), 0)
    ci = lax.broadcasted_iota(jnp.int32, (CHUNK, CHUNK), 1)
    blk16 = (ri // SOLVE_BASE) == (ci // SOLVE_BASE)
    blk32 = (ri // (2 * SOLVE_BASE)) == (ci // (2 * SOLVE_BASE))

    zero = jnp.zeros((HALO, DN_DK), F32)
    for t, src in enumerate((q_ref, k_ref, v_ref)):
        for hh in range(heads):
            pad_ref[t, hh, 0:HALO] = zero
            pad_ref[t, hh, HALO + seq_len:2 * HALO + seq_len] = zero
            pad_ref[t, hh, HALO:HALO + seq_len] = src[:, hh * DN_DK:(hh + 1) * DN_DK]
    o_ref[...] = jnp.zeros_like(o_ref)
    st_ref[...] = s0_ref[...]

    def conv(t, cw_ref, r0, hh):
        acc = None
        for j in range(DN_CONV):
            term = (pad_ref[t, hh, pl.ds(r0 + (HALO - DN_CONV // 2 + j), CHUNK)]
                    * cw_ref[j:j + 1, hh * DN_DK:(hh + 1) * DN_DK])
            acc = term if acc is None else acc + term
        return _silu(acc)

    def l2n(x):
        return x * lax.rsqrt(jnp.sum(x * x, axis=-1, keepdims=True) + EPS)

    lane = lax.broadcasted_iota(jnp.int32, (CHUNK, LANES), 1)

    def gate_col(gblk, col):
        return jnp.sum(jnp.where(lane == col, gblk, 0.0), axis=-1, keepdims=True)

    directions = ((lower, strict_lower, CHUNK - 1), (upper, strict_upper, 0))

    gate_row = lax.broadcasted_iota(jnp.int32, (N_GATE, CHUNK), 0)

    def prep_chain(c, hh, d, q, k, v, gblk, gcum, grow):
        mask, strict, tr = directions[d]
        beta = gate_col(gblk, h0 + hh + (2 + d) * DN_HEADS)
        gc = jnp.broadcast_to(gate_col(gcum[d], h0 + hh + d * DN_HEADS), (CHUNK, DN_DK))
        gr = jnp.sum(jnp.where(gate_row == h0 + hh + d * DN_HEADS, grow[d], 0.0), axis=0, keepdims=True)
        k_beta = k * beta
        kq = _dot_nt(jnp.concatenate([k_beta, q], axis=0).astype(BF16), k.astype(BF16))
        yield
        decay = jnp.exp(jnp.where(mask, gc[:, :CHUNK] - gr, -jnp.inf))
        a = jnp.where(strict, kq[:CHUNK] * decay, 0.0)
        qk = kq[CHUNK:] * decay
        eg = jnp.exp(gc)
        rhs = jnp.concatenate([v * beta, k_beta * eg], axis=-1)
        s = yield from _unit_tri_inverse(a, blk16, blk32)
        z = _dot(s.astype(BF16), rhs.astype(BF16))
        yield
        r = rhs + z
        tot = gc[tr:tr + 1]
        idx = (2 * hh + d) * n + c
        u_ref[idx] = r[:, :DN_DV]
        wq_ref[idx] = jnp.concatenate([r[:, DN_DV:], q * eg], axis=0).astype(BF16)
        qk_ref[idx] = qk.astype(BF16)
        ke_ref[idx] = (k * jnp.exp(tot - gc)).astype(BF16)
        dl_ref[idx] = jnp.broadcast_to(jnp.exp(tot), (SUBLANES, DN_DV))

    def prep(cg, carry):
        chains = []
        for uu in range(unroll):
            c = cg * unroll + uu
            r0 = pl.multiple_of(c * CHUNK, CHUNK)
            gblk = g_ref[pl.ds(r0, CHUNK)]
            gtblk = gt_ref[c]
            gcum = (_mask_dot(lower_bf, gblk), _mask_dot(upper_bf, gblk))
            grow = (_dot_mask(gtblk, upper_bf), _dot_mask(gtblk, lower_bf))
            for hh in range(heads):
                q = l2n(conv(0, cwq_ref, r0, hh)) * (DN_DK ** -0.5)
                k = l2n(conv(1, cwk_ref, r0, hh))
                v = conv(2, cwv_ref, r0, hh)
                for d in range(2):
                    chains.append(prep_chain(c, hh, d, q, k, v, gblk, gcum, grow))
        _interleave(chains)
        return carry

    lax.fori_loop(0, n // unroll, prep, 0)

    def scan_chain(c, hh, d):
        cc = c if d == 0 else n - 1 - c
        idx = (2 * hh + d) * n + cc
        s = st_ref[0, d, hh]
        rows = pl.ds(pl.multiple_of(cc * CHUNK, CHUNK), CHUNK)
        cols = slice(hh * DN_DV, (hh + 1) * DN_DV)
        z = _dot(wq_ref[idx], s.astype(BF16))
        yield
        v_new = (u_ref[idx] - z[:CHUNK]).astype(BF16)
        zo = _dot(qk_ref[idx], v_new)
        zs = _dot_tn(ke_ref[idx], v_new)
        yield
        o_ref[rows, cols] += z[CHUNK:] + zo
        st_ref[0, d, hh] = s * dl_ref[idx][0:1] + zs

    def scan(c, carry):
        _interleave([scan_chain(c, hh, d) for hh in range(heads) for d in range(2)])
        return carry

    lax.fori_loop(0, n, scan, 0)


def _deltanet(qkv, conv_w, gates, gates_t, s0, *, seq_len, n_seq, row_off, heads, unroll):
    off = row_off // seq_len
    n = seq_len // CHUNK
    groups = DN_HEADS // heads
    width = heads * DN_DK
    col = lambda t: pl.BlockSpec((seq_len, width), lambda s, p: (s + off, t * groups + p))
    cw = lambda t: pl.BlockSpec((DN_CONV, width), lambda s, p: (0, t * groups + p))
    st_spec = pl.BlockSpec((1, 2, heads, DN_DK, DN_DV), lambda s, p: (s, 0, p, 0, 0))
    slots = 2 * heads * n
    return pl.pallas_call(
        functools.partial(_dn_kernel, seq_len=seq_len, heads=heads, unroll=unroll),
        out_shape=[jax.ShapeDtypeStruct((n_seq * seq_len, DN_HEADS * DN_DV), F32),
                   jax.ShapeDtypeStruct((n_seq, 2, DN_HEADS, DN_DK, DN_DV), F32)],
        grid=(n_seq, groups),
        in_specs=[col(0), col(1), col(2), cw(0), cw(1), cw(2),
                  pl.BlockSpec((seq_len, LANES), lambda s, p: (s + off, 0)),
                  pl.BlockSpec((n, N_GATE, CHUNK), lambda s, p: (s + off, 0, 0)),
                  st_spec],
        out_specs=[pl.BlockSpec((seq_len, width), lambda s, p: (s, p)), st_spec],
        scratch_shapes=[pltpu.VMEM((3, heads, seq_len + 2 * HALO, DN_DK), F32),
                        pltpu.VMEM((slots, 2 * CHUNK, DN_DK), BF16),
                        pltpu.VMEM((slots, CHUNK, DN_DV), F32),
                        pltpu.VMEM((slots, CHUNK, CHUNK), BF16),
                        pltpu.VMEM((slots, CHUNK, DN_DK), BF16),
                        pltpu.VMEM((slots, SUBLANES, DN_DV), F32)],
        compiler_params=_params(),
        name=f"deltanet_{seq_len}",
    )(qkv, qkv, qkv, conv_w, conv_w, conv_w, gates, gates_t, s0)


def _l1_out_kernel(oc_ref, ol_ref, z_ref, on_ref, w_ref, x_ref, mod_ref, n2_ref, rh_ref, rl_ref,
                   x3_ref, h4_ref, route_ref):
    m = mod_ref[0]
    on = on_ref[...]
    parts = []
    for hd in range(DN_HEADS):
        cs = slice(hd * DN_DV, (hd + 1) * DN_DV)
        parts.append((_rms(_pick_rows(oc_ref, ol_ref, cs), on) * _silu(z_ref[:, cs])).astype(BF16))
    y = _dot(jnp.concatenate(parts, axis=-1), w_ref[...])
    x3 = x_ref[...] + m[2:3] * y
    x3_ref[...] = x3
    h4 = _rms(x3, n2_ref[...]) * (1.0 + m[4:5]) + m[3:4]
    _store_row_tiles(h4_ref, h4)
    h4_hi = h4.astype(BF16)
    h4_lo = (h4 - h4_hi.astype(F32)).astype(BF16)
    logits = (_dot(h4_hi, rh_ref[...]) + _dot(h4_hi, rl_ref[...])) + _dot(h4_lo, rh_ref[...])
    lane = lax.broadcasted_iota(jnp.int32, logits.shape, 1).astype(F32)
    logits = jnp.where(lane < N_EXPERTS, logits, -jnp.inf)
    m1 = jnp.max(logits, axis=-1, keepdims=True)
    i1 = jnp.min(jnp.where(logits == m1, lane, float(LANES)), axis=-1, keepdims=True)
    rest = jnp.where(lane == i1, -jnp.inf, logits)
    m2 = jnp.max(rest, axis=-1, keepdims=True)
    i2 = jnp.min(jnp.where(rest == m2, lane, float(LANES)), axis=-1, keepdims=True)
    e2 = jnp.exp(m2 - m1)
    g1 = 1.0 / (1.0 + e2)
    g2 = e2 / (1.0 + e2)
    route_ref[...] = jnp.where(lane == 0, i1, jnp.where(lane == 1, i2, jnp.where(
        lane == 2, g1, jnp.where(lane == 3, g2, 0.0))))


def _l1_out(o_dn_c, o_dn_l, z, onorm, w, x2, mod, norm2, r_hi, r_lo):
    row = lambda n: pl.BlockSpec((TM, n), lambda i: (i, 0))
    full = lambda a: pl.BlockSpec(a.shape, lambda i: (0,) * a.ndim)
    return pl.pallas_call(
        _l1_out_kernel,
        out_shape=[jax.ShapeDtypeStruct((N_TOK, D_MODEL), F32),
                   jax.ShapeDtypeStruct((N_TOK * ROW_TILES, LANES), F32),
                   jax.ShapeDtypeStruct((N_TOK, LANES), F32)],
        grid=(N_TILES,),
        in_specs=[_ctx_rows(D_MODEL), _lat_rows(D_MODEL), row(D_MODEL), full(onorm), full(w),
                  row(D_MODEL), pl.BlockSpec((1, N_MOD, D_MODEL), lambda i: (_mod_index(i), 0, 0)),
                  full(norm2), full(r_hi), full(r_lo)],
        out_specs=[row(D_MODEL), pl.BlockSpec((TM * ROW_TILES, LANES), lambda i: (i, 0)), row(LANES)],
        compiler_params=_params(),
        name="l1_out_proj_router",
    )(o_dn_c, o_dn_l, z, onorm, w, x2, mod, norm2, r_hi, r_lo)


ROW_DMA_UNROLL = 8


def _moe_kernel(te_ref, nt_ref, nv_ref, src_ref, dst_ref, h_hbm, w1_ref, w3_ref, w2_ref, y_hbm,
                xs_ref, xb_ref, acc_ref, yb_ref, gsem, ssem):
    i = pl.program_id(0)
    j = pl.program_id(1)
    n_used = nt_ref[0]
    slot = i % 2

    def for_valid_rows(tile, fn):
        nv = nv_ref[tile]
        groups = nv // ROW_DMA_UNROLL

        def group(g, carry):
            for u in range(ROW_DMA_UNROLL):
                fn(g * ROW_DMA_UNROLL + u)
            return carry

        def single(r, carry):
            fn(r)
            return carry

        lax.fori_loop(0, groups, group, 0)
        lax.fori_loop(groups * ROW_DMA_UNROLL, nv, single, 0)

    def row_tile(n):
        return pl.ds(pl.multiple_of(n * ROW_TILES, ROW_TILES), ROW_TILES)

    def gather(tile, sl, start):
        def row(r):
            cp = pltpu.make_async_copy(h_hbm.at[row_tile(src_ref[tile * MOE_TM + r])],
                                       xs_ref.at[sl, row_tile(r)], gsem.at[sl])
            cp.start() if start else cp.wait()
        for_valid_rows(tile, row)

    def scatter(tile, sl, start):
        def row(r):
            cp = pltpu.make_async_copy(yb_ref.at[sl, row_tile(r)],
                                       y_hbm.at[row_tile(dst_ref[tile * MOE_TM + r])], ssem.at[sl])
            cp.start() if start else cp.wait()
        for_valid_rows(tile, row)

    @pl.when((i < n_used) & (j == 0))
    def _():
        @pl.when(i == 0)
        def _():
            xs_ref[...] = jnp.zeros_like(xs_ref)
            gather(i, slot, True)
        gather(i, slot, False)

        @pl.when(i + 1 < n_used)
        def _():
            gather(i + 1, 1 - slot, True)
        xb_ref[...] = _load_row_tiles(xs_ref, MOE_TM, (slot,)).astype(BF16)
        acc_ref[...] = jnp.zeros_like(acc_ref)

    @pl.when(i < n_used)
    def _():
        x = xb_ref[...]
        g = (_silu(_dot(x, w1_ref[0])) * _dot(x, w3_ref[0])).astype(BF16)
        acc_ref[...] += _dot(g, w2_ref[0])

    @pl.when((i < n_used) & (j == pl.num_programs(1) - 1))
    def _():
        @pl.when(i >= 2)
        def _():
            scatter(i - 2, slot, False)
        _store_row_tiles(yb_ref, acc_ref[...], (slot,))
        scatter(i, slot, True)

    @pl.when((i == pl.num_programs(0) - 1) & (j == pl.num_programs(1) - 1))
    def _():
        @pl.when(n_used >= 2)
        def _():
            scatter(n_used - 2, n_used % 2, False)
        scatter(n_used - 1, (n_used - 1) % 2, False)


def _moe(tile_expert, n_used, n_valid, src_rows, dst_rows, h4, w1, w3, w2):
    def wmap(i, j, te, nt, nv, src, dst):
        return (te[i], 0, jnp.where(i < nt[0], j, MOE_FF_STEPS - 1))

    def w2map(i, j, te, nt, nv, src, dst):
        return (te[i], jnp.where(i < nt[0], j, MOE_FF_STEPS - 1), 0)

    grid_spec = pltpu.PrefetchScalarGridSpec(
        num_scalar_prefetch=5,
        grid=(MOE_TILES, MOE_FF_STEPS),
        in_specs=[pl.BlockSpec(memory_space=pl.ANY),
                  pl.BlockSpec((1, D_MODEL, MOE_FF), wmap),
                  pl.BlockSpec((1, D_MODEL, MOE_FF), wmap),
                  pl.BlockSpec((1, MOE_FF, D_MODEL), w2map)],
        out_specs=pl.BlockSpec(memory_space=pl.ANY),
        scratch_shapes=[pltpu.VMEM((2, MOE_TM * ROW_TILES, LANES), F32),
                        pltpu.VMEM((MOE_TM, D_MODEL), BF16),
                        pltpu.VMEM((MOE_TM, D_MODEL), F32),
                        pltpu.VMEM((2, MOE_TM * ROW_TILES, LANES), F32),
                        pltpu.SemaphoreType.DMA((2,)),
                        pltpu.SemaphoreType.DMA((2,))])
    return pl.pallas_call(
        _moe_kernel,
        out_shape=jax.ShapeDtypeStruct((2 * N_TOK * ROW_TILES, LANES), F32),
        grid_spec=grid_spec,
        compiler_params=_params(has_side_effects=True),
        name="routed_experts",
    )(tile_expert, n_used, n_valid, src_rows, dst_rows, h4, w1, w3, w2)


def _combine_kernel(y0_ref, y1_ref, route_ref, x_ref, mod_ref, fn_ref, o_ref):
    m = mod_ref[0]
    route = route_ref[...]
    y = (route[:, 2:3] * _load_row_tiles(y0_ref, TM)) + (route[:, 3:4] * _load_row_tiles(y1_ref, TM))
    o_ref[...] = _rms(x_ref[...] + m[5:6] * y, fn_ref[...])


def _combine(y, route, x3, mod, final_norm, *, first_tile, n_tiles):
    row = lambda n: pl.BlockSpec((TM, n), lambda i: (i + first_tile, 0))
    slot = lambda k: pl.BlockSpec((TM * ROW_TILES, LANES), lambda i: (i + first_tile + k * N_TILES, 0))
    return pl.pallas_call(
        _combine_kernel,
        out_shape=jax.ShapeDtypeStruct((n_tiles * TM, D_MODEL), F32),
        grid=(n_tiles,),
        in_specs=[slot(0), slot(1), row(LANES), row(D_MODEL),
                  pl.BlockSpec((1, N_MOD, D_MODEL), lambda i: (_mod_index(i + first_tile), 0, 0)),
                  pl.BlockSpec((1, D_MODEL), lambda i: (0, 0))],
        out_specs=pl.BlockSpec((TM, D_MODEL), lambda i: (i, 0)),
        compiler_params=_params(),
        name="combine_final_norm",
    )(y, y, route, x3, mod, final_norm)


def _rope_tables():
    rows = LAT_LEN // GRID_W
    row = jnp.repeat(jnp.arange(rows), GRID_W).astype(F32)
    col = (jnp.arange(rows * GRID_W) % GRID_W).astype(F32)
    half = SWA_HD // 2
    inv_freq = ROPE_BASE ** (-jnp.arange(0, half, 2, dtype=F32) / half)
    ang_r = row[:, None] * inv_freq
    ang_c = col[:, None] * inv_freq
    cos = jnp.concatenate([jnp.cos(ang_r)] * 2 + [jnp.cos(ang_c)] * 2, axis=-1)
    sin = jnp.concatenate([-jnp.sin(ang_r), jnp.sin(ang_r), -jnp.sin(ang_c), jnp.sin(ang_c)], axis=-1)
    cos = jnp.tile(cos, (1, SWA_HEADS))
    sin = jnp.tile(sin, (1, SWA_HEADS))
    cos = jnp.concatenate([cos, jnp.ones((TM, cos.shape[1]), F32)], axis=0)
    sin = jnp.concatenate([sin, jnp.zeros((TM, sin.shape[1]), F32)], axis=0)
    return cos, sin


def _dispatch(route):
    e = jnp.concatenate([route[:, 0], route[:, 1]]).astype(jnp.int32)
    onehot = (e[:, None] == jnp.arange(N_EXPERTS)[None, :]).astype(jnp.int32)
    counts = jnp.sum(onehot, axis=0)
    rank = jnp.sum((jnp.cumsum(onehot, axis=0) - onehot) * onehot, axis=1)
    padded = ((counts + MOE_TM - 1) // MOE_TM) * MOE_TM
    ends = jnp.cumsum(padded)
    starts = ends - padded
    pos = starts[e] + rank
    dst = jnp.zeros((MOE_ROWS,), jnp.int32).at[pos].set(jnp.arange(2 * N_TOK, dtype=jnp.int32))
    src = dst % N_TOK
    n_used = (ends[-1] // MOE_TM).astype(jnp.int32)
    tile_start = jnp.arange(MOE_TILES, dtype=jnp.int32) * MOE_TM
    tile_expert = jnp.minimum(jnp.sum(tile_start[:, None] >= ends[None, :], axis=1), N_EXPERTS - 1)
    n_valid = jnp.clip((starts + counts)[tile_expert] - tile_start, 0, MOE_TM).astype(jnp.int32)
    last = jnp.take(tile_expert, jnp.maximum(n_used - 1, 0))
    tile_expert = jnp.where(jnp.arange(MOE_TILES) < n_used, tile_expert, last).astype(jnp.int32)
    return tile_expert, n_used.reshape(1), n_valid, src, dst


def kernel(x_prompt, x_sample, c, state_gla_l0, cache_k_l0, cache_v_l0, state_dn_l1, c_ctx, final_norm, ada_w_l0, ada_b_l0, norm1_l0, norm2_l0, in_w_l0, gla_w2_l0, gla_b_l0, gla_onorm_l0, sink_l0, out_w_l0, ffn_w1_l0, ffn_w3_l0, ffn_w2_l0, ada_w_l1, ada_b_l1, norm1_l1, norm2_l1, in_w_l1, conv_w_l1, a_log_l1, dt_bias_l1, dn_onorm_l1, out_w_l1, router_l1, moe_w1_l1, moe_w3_l1, moe_w2_l1):
    D = D_MODEL
    x_ctx = x_prompt.reshape(N_CTX_TOK, D)
    x_lat = x_sample.reshape(N_LAT_TOK, D)
    row2 = lambda v: v.reshape(1, -1)

    cond = jnp.concatenate([c_ctx[None], c, jnp.zeros((2 * SUBLANES - 1 - N_LAT_SEQ, D), F32)], axis=0)
    mod0 = _modulation(cond, ada_w_l0, ada_b_l0).reshape(-1, N_MOD, D)
    mod1 = _modulation(cond, ada_w_l1, ada_b_l1).reshape(-1, N_MOD, D)

    gq_w, gk_w, gv_w, gr_w, glrf_w, glrb_w, sq_w, sk_w, sv_w = jnp.split(
        in_w_l0, [256, 512, 1024, 1536, 1552, 1568, 2080, 2208], axis=1)
    w0 = jnp.concatenate([gq_w, gk_w, gv_w, gr_w, sq_w, sk_w, sv_w, glrf_w, glrb_w,
                          jnp.zeros((D, L0_END - L0_GLR - 2 * GLA_RANK), F32)], axis=1).astype(BF16)
    w2f = jnp.zeros((LANES, GLA_HEADS * GLA_DK), F32).at[:GLA_RANK].set(gla_w2_l0[0]).astype(BF16)
    w2b = jnp.zeros((LANES, GLA_HEADS * GLA_DK), F32).at[GLA_RANK:2 * GLA_RANK].set(gla_w2_l0[1]).astype(BF16)
    cos, sin = _rope_tables()
    gq, gk, gv, gr, laf, lab, sq, sk, sv = _l0_in(x_ctx, x_lat, mod0, row2(norm1_l0), w0, w2f, w2b,
                                                  gla_b_l0, cos, sin)

    onorm0 = row2(gla_onorm_l0)
    zero_state = jnp.zeros((N_CTX_SEQ, 2, GLA_HEADS, GLA_DV, GLA_DK), F32)
    o_gla_c, st_gla = _gla(gq, gk, gv, gr, laf, lab, zero_state, onorm0,
                           seq_len=CTX_LEN, n_seq=N_CTX_SEQ, row_off=0)
    o_gla_l, _ = _gla(gq, gk, gv, gr, laf, lab, jnp.swapaxes(state_gla_l0, -1, -2), onorm0,
                      seq_len=LAT_LEN, n_seq=N_LAT_SEQ, row_off=N_CTX_TOK)
    o_swa_c = _swa_ctx(sink_l0, sq, sk, sv)
    o_swa_l = _swa_lat(sink_l0, sq, sk, sv,
                       cache_k_l0.reshape(N_LAT_SEQ, -1, SWA_KV_HEADS * SWA_HD),
                       cache_v_l0.reshape(N_LAT_SEQ, -1, SWA_KV_HEADS * SWA_HD))
    x1, h2 = _l0_out(o_gla_c, o_gla_l, o_swa_c, o_swa_l, out_w_l0.astype(BF16), x_ctx, x_lat, mod0,
                     row2(norm2_l0))

    x2, h3 = _ffn(h2, x1, ffn_w1_l0.astype(BF16), ffn_w3_l0.astype(BF16), ffn_w2_l0.astype(BF16),
                  mod0, mod1, row2(norm1_l1))

    w_ab = in_w_l1[:, L1_AB:]
    wl1 = jnp.concatenate([in_w_l1, jnp.zeros((D, L1_END - in_w_l1.shape[1]), F32)], axis=1).astype(BF16)
    zeros_g = jnp.zeros((2 * DN_HEADS,), F32)
    alog = jnp.concatenate([a_log_l1.reshape(-1), zeros_g])
    dtb = jnp.concatenate([dt_bias_l1.reshape(-1), zeros_g])
    pad_lane = lambda v: jnp.concatenate([v, jnp.zeros((LANES - N_GATE,), F32)]).reshape(1, LANES)
    qkv, z, gates, gates_t = _l1_in(h3, wl1, w_ab.T.astype(BF16), pad_lane(alog), pad_lane(dtb),
                                    alog.reshape(N_GATE, 1), dtb.reshape(N_GATE, 1))
    gates_t = gates_t.reshape(N_GATE, N_TOK // CHUNK, CHUNK).transpose(1, 0, 2)
    zero_dn = jnp.zeros((N_CTX_SEQ, 2, DN_HEADS, DN_DK, DN_DV), F32)
    o_dn_c, st_dn = _deltanet(qkv, conv_w_l1, gates, gates_t, zero_dn,
                              seq_len=CTX_LEN, n_seq=N_CTX_SEQ, row_off=0, heads=4, unroll=2)
    o_dn_l, _ = _deltanet(qkv, conv_w_l1, gates, gates_t, state_dn_l1,
                          seq_len=LAT_LEN, n_seq=N_LAT_SEQ, row_off=N_CTX_TOK, heads=2, unroll=4)
    router = jnp.concatenate([router_l1, jnp.zeros((D, LANES - N_EXPERTS), F32)], axis=1)
    r_hi = router.astype(BF16)
    r_lo = (router - r_hi.astype(F32)).astype(BF16)
    x3, h4, route = _l1_out(o_dn_c, o_dn_l, z, row2(dn_onorm_l1), out_w_l1.astype(BF16), x2, mod1,
                            row2(norm2_l1), r_hi, r_lo)

    tile_expert, n_used, n_valid, src, dst = _dispatch(route)
    y = _moe(tile_expert, n_used, n_valid, src, dst, h4,
             moe_w1_l1.astype(BF16), moe_w3_l1.astype(BF16), moe_w2_l1.astype(BF16))
    fn = row2(final_norm)
    y_prompt = _combine(y, route, x3, mod1, fn, first_tile=0, n_tiles=CTX_TILES)
    y_sample = _combine(y, route, x3, mod1, fn, first_tile=CTX_TILES, n_tiles=N_TILES - CTX_TILES)
    y_prompt = y_prompt.reshape(N_CTX_SEQ, CTX_LEN, D)
    y_sample = y_sample.reshape(N_LAT_SEQ, LAT_LEN, D)
    new_state_gla = jnp.swapaxes(st_gla, -1, -2)
    new_k = sk[:N_CTX_TOK].reshape(N_CTX_SEQ, CTX_LEN, SWA_KV_HEADS, SWA_HD)
    new_v = sv[:N_CTX_TOK].reshape(N_CTX_SEQ, CTX_LEN, SWA_KV_HEADS, SWA_HD)
    return (y_prompt, y_sample, new_state_gla, new_k, new_v, st_dn)
```

```python
import functools
import math

import jax
import jax.numpy as jnp
from jax import lax
from jax.experimental import pallas as pl
from jax.experimental.pallas import tpu as pltpu

F32 = jnp.float32
BF16 = jnp.bfloat16

D_MODEL = 1024
N_CTX_SEQ = 16
CTX_LEN = 256
N_LAT_SEQ = 8
LAT_LEN = 2048
N_CTX_TOK = N_CTX_SEQ * CTX_LEN
N_LAT_TOK = N_LAT_SEQ * LAT_LEN
N_TOK = N_CTX_TOK + N_LAT_TOK
GRID_W = 64
EPS = 1e-6
N_MOD = 6

GLA_HEADS = 4
GLA_DK = 64
GLA_DV = 128
GLA_RANK = 16
GLA_TAU = 16.0
CHUNK = 64
GLA_UNROLL = 8

SWA_HEADS = 8
SWA_KV_HEADS = 2
SWA_GROUP = SWA_HEADS // SWA_KV_HEADS
SWA_HD = 64
SWA_BLOCK = 128
ROPE_BASE = 10000.0

DN_HEADS = 8
DN_DK = 128
DN_DV = 128
DN_CONV = 5
DN_QKV = 3 * DN_HEADS * DN_DK

D_FF_DENSE = 2816
N_EXPERTS = 8
D_FF_EXPERT = 3584

LANES = 128
SUBLANES = 8
MXU_DIM = 256
VMEM_LIMIT = 56 * 1024 * 1024

TM = 512
N_TILES = N_TOK // TM
CTX_TILES = N_CTX_TOK // TM
LAT_TILES_PER_SEQ = LAT_LEN // TM
FF_CHUNK = MXU_DIM
MOE_TM = 512
MOE_FF = 1792
MOE_ROWS = 2 * N_TOK + N_EXPERTS * MOE_TM
MOE_TILES = MOE_ROWS // MOE_TM
MOE_FF_STEPS = D_FF_EXPERT // MOE_FF


def _params(**kw):
    return pltpu.CompilerParams(vmem_limit_bytes=VMEM_LIMIT, **kw)


def _sigmoid(x):
    return 1.0 / (1.0 + jnp.exp(-x))


def _silu(x):
    return x * _sigmoid(x)


def _softplus(x):
    return jnp.maximum(x, 0.0) + jnp.log(1.0 + jnp.exp(-jnp.abs(x)))


def _rms(x, g):
    return x * lax.rsqrt(jnp.mean(x * x, axis=-1, keepdims=True) + EPS) * g


def _dot(a, b):
    return jnp.dot(a, b, preferred_element_type=F32)


def _dot_nt(a, b):
    return lax.dot_general(a, b, (((1,), (1,)), ((), ())), preferred_element_type=F32)


def _dot_tn(a, b):
    return lax.dot_general(a, b, (((0,), (0,)), ((), ())), preferred_element_type=F32)


def _split3(x):
    hi = x.astype(BF16)
    r = x - hi.astype(F32)
    mid = r.astype(BF16)
    lo = (r - mid.astype(F32)).astype(BF16)
    return hi, mid, lo


def _mask_dot(mask_bf, x):
    hi, mid, lo = _split3(x)
    return (_dot(mask_bf, hi) + _dot(mask_bf, mid)) + _dot(mask_bf, lo)


def _dot_mask(x, mask_bf):
    hi, mid, lo = _split3(x)
    return (_dot(hi, mask_bf) + _dot(mid, mask_bf)) + _dot(lo, mask_bf)


def _interleave(chains):
    results = [None] * len(chains)
    active = list(enumerate(chains))
    while active:
        still = []
        for i, chain in active:
            try:
                next(chain)
                still.append((i, chain))
            except StopIteration as stop:
                results[i] = stop.value
        active = still
    return results


def _tri_masks(n):
    r = lax.broadcasted_iota(jnp.int32, (n, n), 0)
    c = lax.broadcasted_iota(jnp.int32, (n, n), 1)
    return c <= r, c >= r


def _ctx_rows(w):
    return pl.BlockSpec((TM, w), lambda i: (jnp.minimum(i, CTX_TILES - 1), 0))


def _lat_rows(w):
    return pl.BlockSpec((TM, w), lambda i: (jnp.maximum(i - CTX_TILES, 0), 0))


def _pick_rows(ctx_ref, lat_ref, cols=slice(None)):
    return jnp.where(pl.program_id(0) < CTX_TILES, ctx_ref[:, cols], lat_ref[:, cols])


ROW_TILES = D_MODEL // LANES


def _store_row_tiles(ref, x, lead=()):
    rows = x.shape[0]
    for s in range(ROW_TILES):
        ref[lead + (pl.ds(s, rows, stride=ROW_TILES), slice(None))] = x[:, s * LANES:(s + 1) * LANES]


def _load_row_tiles(ref, rows, lead=()):
    return jnp.concatenate([ref[lead + (pl.ds(s, rows, stride=ROW_TILES), slice(None))]
                            for s in range(ROW_TILES)], axis=-1)


def _mod_index(i):
    return jnp.where(i < CTX_TILES, 0, 1 + (i - CTX_TILES) // LAT_TILES_PER_SEQ)


def _mod_kernel(c_ref, w_ref, b_ref, o_ref):
    s = _silu(c_ref[...]).astype(BF16)
    o_ref[...] = _dot(s, w_ref[...].astype(BF16)) + b_ref[...]


def _modulation(cond, w, b):
    m, d = cond.shape
    n = w.shape[1]
    tn = n // 4
    return pl.pallas_call(
        _mod_kernel,
        out_shape=jax.ShapeDtypeStruct((m, n), F32),
        grid=(n // tn,),
        in_specs=[pl.BlockSpec((m, d), lambda j: (0, 0)),
                  pl.BlockSpec((d, tn), lambda j: (0, j)),
                  pl.BlockSpec((1, tn), lambda j: (0, j))],
        out_specs=pl.BlockSpec((m, tn), lambda j: (0, j)),
        compiler_params=_params(),
        name="modulation",
    )(cond, w, b.reshape(1, n))


L0_GQ, L0_GK, L0_GV, L0_GR, L0_SQ, L0_SK, L0_SV, L0_GLR, L0_END = (
    0, 256, 512, 1024, 1536, 2048, 2176, 2304, 2432)


def _rope(x, cos, sin):
    n = x.shape[-1]
    lane = lax.broadcasted_iota(jnp.int32, x.shape, 1)
    first = (lane % 32) < 16
    partner = jnp.where(first, pltpu.roll(x, n - 16, 1), pltpu.roll(x, 16, 1))
    return x * cos + partner * sin


def _log_sigmoid(x):
    return jnp.minimum(x, 0.0) - jnp.log(1.0 + jnp.exp(-jnp.abs(x)))


def _l0_in_kernel(xc_ref, xl_ref, mod_ref, n1_ref, w_ref, w2f_ref, w2b_ref, gb_ref, cos_ref, sin_ref,
                  gq_ref, gk_ref, gv_ref, gr_ref, laf_ref, lab_ref, sq_ref, sk_ref, sv_ref):
    m = mod_ref[0]
    h = _rms(_pick_rows(xc_ref, xl_ref), n1_ref[...]) * (1.0 + m[1:2]) + m[0:1]
    hb = h.astype(BF16)

    def proj(a, b):
        return _dot(hb, w_ref[:, a:b])

    gq_ref[...] = proj(L0_GQ, L0_GK) * (GLA_DK ** -0.5)
    gk_ref[...] = proj(L0_GK, L0_GV)
    gv_ref[...] = proj(L0_GV, L0_GR)
    gr_ref[...] = proj(L0_GR, L0_SQ)
    cos = cos_ref[...]
    sin = sin_ref[...]
    sq_ref[...] = _rope(proj(L0_SQ, L0_SK), cos, sin)
    nk = L0_SV - L0_SK
    sk_ref[...] = _rope(proj(L0_SK, L0_SV), cos[:, :nk], sin[:, :nk])
    sv_ref[...] = proj(L0_SV, L0_GLR)
    glr = proj(L0_GLR, L0_END).astype(BF16)
    gb = gb_ref[...]
    laf_ref[...] = _log_sigmoid(_dot(glr, w2f_ref[...]) + gb[0:1]) * (1.0 / GLA_TAU)
    lab_ref[...] = _log_sigmoid(_dot(glr, w2b_ref[...]) + gb[1:2]) * (1.0 / GLA_TAU)


def _rope_index(i):
    return jnp.where(i < CTX_TILES, LAT_TILES_PER_SEQ, (i - CTX_TILES) % LAT_TILES_PER_SEQ)


def _l0_in(x_ctx, x_lat, mod, norm1, w, w2f, w2b, gb, cos, sin):
    row = lambda n: pl.BlockSpec((TM, n), lambda i: (i, 0))
    full = lambda a: pl.BlockSpec(a.shape, lambda i: (0,) * a.ndim)
    widths = (256, 256, 512, 512, 256, 256, 512, 128, 128)
    return pl.pallas_call(
        _l0_in_kernel,
        out_shape=[jax.ShapeDtypeStruct((N_TOK, n), F32) for n in widths],
        grid=(N_TILES,),
        in_specs=[_ctx_rows(D_MODEL), _lat_rows(D_MODEL),
                  pl.BlockSpec((1, N_MOD, D_MODEL), lambda i: (_mod_index(i), 0, 0)),
                  full(norm1), full(w), full(w2f), full(w2b), full(gb),
                  pl.BlockSpec((TM, 512), lambda i: (_rope_index(i), 0)),
                  pl.BlockSpec((TM, 512), lambda i: (_rope_index(i), 0))],
        out_specs=[row(n) for n in widths],
        compiler_params=_params(),
        name="l0_in_proj",
    )(x_ctx, x_lat, mod, norm1, w, w2f, w2b, gb, cos, sin)


def _gla_kernel(q_ref, k_ref, v_ref, r_ref, laf_ref, lab_ref, s0_ref, on_ref,
                o_ref, st_ref, oacc_ref, *, seq_len):
    n = seq_len // CHUNK
    lower, upper = _tri_masks(CHUNK)
    lower_bf = lower.astype(BF16)
    upper_bf = upper.astype(BF16)
    oacc_ref[...] = jnp.zeros_like(oacc_ref)
    st_ref[...] = s0_ref[...]

    def chain(d, chunks, la_ref, mask_bf, mask, tot_row):
        rows = [pl.ds(pl.multiple_of(c * CHUNK, CHUNK), CHUNK) for c in chunks]
        cums = [_mask_dot(mask_bf, la_ref[r, :]) for r in rows]
        yield
        parts = []
        for r, b2 in zip(rows, cums):
            for hh in range(2):
                ck = slice(hh * GLA_DK, (hh + 1) * GLA_DK)
                cv = slice(hh * GLA_DV, (hh + 1) * GLA_DV)
                b = b2[:, ck]
                tot = b[tot_row:tot_row + 1]
                q = q_ref[r, ck]
                k = k_ref[r, ck]
                q_dec = (q * jnp.exp(b)).astype(BF16)
                k_inv = (k * jnp.exp(-b)).astype(BF16)
                k_end = (k * jnp.exp(tot - b)).astype(BF16)
                vb = v_ref[r, cv].astype(BF16)
                parts.append((r, hh, cv, jnp.exp(tot), vb, q_dec, _dot_nt(q_dec, k_inv), _dot_tn(vb, k_end)))
        yield
        states = [st_ref[0, d, hh] for hh in range(2)]
        outs = []
        for r, hh, cv, decay, vb, q_dec, attn, kv in parts:
            o_intra = _dot(jnp.where(mask, attn, 0.0).astype(BF16), vb)
            o_inter = _dot_nt(q_dec, states[hh].astype(BF16))
            states[hh] = states[hh] * decay + kv
            outs.append((r, cv, o_intra, o_inter))
        yield
        for r, cv, o_intra, o_inter in outs:
            oacc_ref[r, cv] += o_intra + o_inter
        for hh in range(2):
            st_ref[0, d, hh] = states[hh]

    unroll = min(GLA_UNROLL, n)

    def body(g, carry):
        fwd = [g * unroll + t for t in range(unroll)]
        bwd = [n - 1 - c for c in fwd]
        _interleave([chain(0, fwd, laf_ref, lower_bf, lower, CHUNK - 1),
                     chain(1, bwd, lab_ref, upper_bf, upper, 0)])
        return carry

    lax.fori_loop(0, n // unroll, body, 0)
    for hh in range(2):
        cv = slice(hh * GLA_DV, (hh + 1) * GLA_DV)
        o_ref[:, cv] = _rms(oacc_ref[:, cv], on_ref[...]) * _silu(r_ref[:, cv])


def _gla(gq, gk, gv, gr, laf, lab, s0t, onorm, *, seq_len, n_seq, row_off):
    off = row_off // seq_len
    blk = lambda w: pl.BlockSpec((seq_len, w), lambda s, p: (s + off, p))
    st_spec = pl.BlockSpec((1, 2, 2, GLA_DV, GLA_DK), lambda s, p: (s, 0, p, 0, 0))
    return pl.pallas_call(
        functools.partial(_gla_kernel, seq_len=seq_len),
        out_shape=[jax.ShapeDtypeStruct((n_seq * seq_len, GLA_HEADS * GLA_DV), F32),
                   jax.ShapeDtypeStruct((n_seq, 2, GLA_HEADS, GLA_DV, GLA_DK), F32)],
        grid=(n_seq, GLA_HEADS // 2),
        in_specs=[blk(2 * GLA_DK), blk(2 * GLA_DK), blk(2 * GLA_DV), blk(2 * GLA_DV),
                  blk(2 * GLA_DK), blk(2 * GLA_DK), st_spec,
                  pl.BlockSpec((1, GLA_DV), lambda s, p: (0, 0))],
        out_specs=[pl.BlockSpec((seq_len, 2 * GLA_DV), lambda s, p: (s, p)), st_spec],
        scratch_shapes=[pltpu.VMEM((seq_len, 2 * GLA_DV), F32)],
        compiler_params=_params(),
        name=f"gla_{seq_len}",
    )(gq, gk, gv, gr, laf, lab, s0t, onorm)


def _attend(q, kcat, vcat, mask, sink_ref):
    def head(h):
        j = h // SWA_GROUP
        cj = slice(j * SWA_HD, (j + 1) * SWA_HD)
        s = _dot_nt(q[:, h * SWA_HD:(h + 1) * SWA_HD].astype(BF16), kcat[:, cj])
        yield
        if mask is not None:
            s = jnp.where(mask, s, -jnp.inf)
        sink = sink_ref[h]
        m = jnp.maximum(jnp.max(s, axis=-1, keepdims=True), sink)
        p = jnp.exp(s - m)
        denom = jnp.sum(p, axis=-1, keepdims=True) + jnp.exp(sink - m)
        o = _dot(p.astype(BF16), vcat[:, cj])
        yield
        return o / denom

    return jnp.concatenate(_interleave([head(h) for h in range(SWA_HEADS)]), axis=-1)


def _swa_ctx_kernel(sink_ref, q_ref, k_ref, v_ref, o_ref):
    o_ref[...] = _attend(q_ref[...] * (SWA_HD ** -0.5), k_ref[...].astype(BF16),
                         v_ref[...].astype(BF16), None, sink_ref)


def _swa_ctx(sink, sq, sk, sv):
    blk = lambda w: pl.BlockSpec((CTX_LEN, w), lambda b: (b, 0))
    return pl.pallas_call(
        _swa_ctx_kernel,
        out_shape=jax.ShapeDtypeStruct((N_CTX_TOK, SWA_HEADS * SWA_HD), F32),
        grid=(N_CTX_SEQ,),
        in_specs=[pl.BlockSpec(memory_space=pltpu.SMEM), blk(512), blk(128), blk(128)],
        out_specs=blk(512),
        compiler_params=_params(),
        name="swa_ctx",
    )(sink, sq, sk, sv)


def _swa_lat_kernel(sink_ref, q_ref, kc_ref, vc_ref, kp_ref, kn_ref, k_ref, vp_ref, vn_ref, v_ref,
                    o_ref):
    i = pl.program_id(1)
    nb = pl.num_programs(1)
    kcat = jnp.concatenate([kc_ref[0], kp_ref[...], k_ref[...], kn_ref[...]], axis=0).astype(BF16)
    vcat = jnp.concatenate([vc_ref[0], vp_ref[...], v_ref[...], vn_ref[...]], axis=0).astype(BF16)
    n_ctx = kc_ref.shape[1]
    n = n_ctx + 3 * SWA_BLOCK
    r = lax.broadcasted_iota(jnp.int32, (SWA_BLOCK, n), 0)
    c = lax.broadcasted_iota(jnp.int32, (SWA_BLOCK, n), 1) - n_ctx
    key_pos = (i - 1) * SWA_BLOCK + c
    dist = SWA_BLOCK + r - c
    mask = (c < 0) | ((jnp.abs(dist) <= SWA_BLOCK) & (key_pos >= 0) & (key_pos < nb * SWA_BLOCK))
    o_ref[...] = _attend(q_ref[...] * (SWA_HD ** -0.5), kcat, vcat, mask, sink_ref)


def _swa_lat(sink, sq, sk, sv, cache_k, cache_v):
    nb = LAT_LEN // SWA_BLOCK
    off = N_CTX_TOK // SWA_BLOCK
    cur = lambda b, i: (off + b * nb + i, 0)
    prev = lambda b, i: (off + b * nb + jnp.maximum(i - 1, 0), 0)
    nxt = lambda b, i: (off + b * nb + jnp.minimum(i + 1, nb - 1), 0)
    kv = lambda f: pl.BlockSpec((SWA_BLOCK, 128), f)
    cache = pl.BlockSpec((1, cache_k.shape[1], 128), lambda b, i: (b, 0, 0))
    return pl.pallas_call(
        _swa_lat_kernel,
        out_shape=jax.ShapeDtypeStruct((N_LAT_TOK, SWA_HEADS * SWA_HD), F32),
        grid=(N_LAT_SEQ, nb),
        in_specs=[pl.BlockSpec(memory_space=pltpu.SMEM),
                  pl.BlockSpec((SWA_BLOCK, 512), cur), cache, cache,
                  kv(prev), kv(nxt), kv(cur), kv(prev), kv(nxt), kv(cur)],
        out_specs=pl.BlockSpec((SWA_BLOCK, 512), lambda b, i: (b * nb + i, 0)),
        compiler_params=_params(),
        name="swa_lat",
    )(sink, sq, cache_k, cache_v, sk, sk, sk, sv, sv, sv)


def _l0_out_kernel(ogc_ref, ogl_ref, osc_ref, osl_ref, w_ref, xc_ref, xl_ref, mod_ref, n2_ref,
                   x1_ref, h2_ref):
    m = mod_ref[0]
    k_half = ogc_ref.shape[1]
    is_ctx = pl.program_id(0) < CTX_TILES

    def rows_of(first_row, n_rows):
        rows = slice(first_row, first_row + n_rows)
        pick = lambda c_ref, l_ref: jnp.where(is_ctx, c_ref[rows, :], l_ref[rows, :])
        y = (_dot(pick(ogc_ref, ogl_ref).astype(BF16), w_ref[:k_half])
             + _dot(pick(osc_ref, osl_ref).astype(BF16), w_ref[k_half:]))
        yield
        x1 = pick(xc_ref, xl_ref) + m[2:3] * y
        x1_ref[rows, :] = x1
        h2_ref[rows, :] = (_rms(x1, n2_ref[...]) * (1.0 + m[4:5]) + m[3:4]).astype(BF16)

    n_rows = x1_ref.shape[0] // 2
    _interleave([rows_of(0, n_rows), rows_of(n_rows, n_rows)])


def _l0_out(o_gla_c, o_gla_l, o_swa_c, o_swa_l, w, x_ctx, x_lat, mod, norm2):
    row = lambda n: pl.BlockSpec((TM, n), lambda i: (i, 0))
    full = lambda a: pl.BlockSpec(a.shape, lambda i: (0,) * a.ndim)
    return pl.pallas_call(
        _l0_out_kernel,
        out_shape=[jax.ShapeDtypeStruct((N_TOK, D_MODEL), F32),
                   jax.ShapeDtypeStruct((N_TOK, D_MODEL), BF16)],
        grid=(N_TILES,),
        in_specs=[_ctx_rows(512), _lat_rows(512), _ctx_rows(512), _lat_rows(512), full(w),
                  _ctx_rows(D_MODEL), _lat_rows(D_MODEL),
                  pl.BlockSpec((1, N_MOD, D_MODEL), lambda i: (_mod_index(i), 0, 0)), full(norm2)],
        out_specs=[row(D_MODEL), row(D_MODEL)],
        compiler_params=_params(),
        name="l0_out_proj",
    )(o_gla_c, o_gla_l, o_swa_c, o_swa_l, w, x_ctx, x_lat, mod, norm2)


def _ffn_kernel(h_ref, x_ref, w1_ref, w3_ref, w2_ref, mod0_ref, mod1_ref, n1_ref, x2_ref, h3_ref):
    h = h_ref[...]
    acc = jnp.zeros((h.shape[0], D_MODEL), F32)
    for j in range(w1_ref.shape[1] // FF_CHUNK):
        cols = slice(j * FF_CHUNK, (j + 1) * FF_CHUNK)
        g = (_silu(_dot(h, w1_ref[:, cols])) * _dot(h, w3_ref[:, cols])).astype(BF16)
        acc = acc + _dot(g, w2_ref[cols, :])
    m0 = mod0_ref[0]
    m1 = mod1_ref[0]
    x2 = x_ref[...] + m0[5:6] * acc
    x2_ref[...] = x2
    h3_ref[...] = (_rms(x2, n1_ref[...]) * (1.0 + m1[1:2]) + m1[0:1]).astype(BF16)


def _ffn(h2, x1, w1, w3, w2, mod0, mod1, norm1_next):
    row = lambda n: pl.BlockSpec((TM, n), lambda i: (i, 0))
    full = lambda a: pl.BlockSpec(a.shape, lambda i: (0,) * a.ndim)
    mod = pl.BlockSpec((1, N_MOD, D_MODEL), lambda i: (_mod_index(i), 0, 0))
    return pl.pallas_call(
        _ffn_kernel,
        out_shape=[jax.ShapeDtypeStruct((N_TOK, D_MODEL), F32),
                   jax.ShapeDtypeStruct((N_TOK, D_MODEL), BF16)],
        grid=(N_TILES,),
        in_specs=[row(D_MODEL), row(D_MODEL), full(w1), full(w3), full(w2), mod, mod,
                  full(norm1_next)],
        out_specs=[row(D_MODEL), row(D_MODEL)],
        compiler_params=_params(),
        name="dense_ffn",
    )(h2, x1, w1, w3, w2, mod0, mod1, norm1_next)


L1_Z = DN_QKV
L1_AB = DN_QKV + DN_HEADS * DN_DV
L1_END = L1_AB + LANES
N_GATE = 4 * DN_HEADS


def _dn_gates(ab, neg_a, dt_bias, is_decay):
    return jnp.where(is_decay, neg_a * _softplus(ab + dt_bias), _sigmoid(ab))


def _l1_in_kernel(h_ref, w_ref, wt_ref, alog_ref, dtb_ref, alogc_ref, dtbc_ref,
                  qkv_ref, z_ref, g_ref, gt_ref):
    h = h_ref[...]
    qkv_ref[...] = _dot(h, w_ref[:, :L1_Z])
    z_ref[...] = _dot(h, w_ref[:, L1_Z:L1_AB])
    ab = _dot(h, w_ref[:, L1_AB:L1_END])
    lane = lax.broadcasted_iota(jnp.int32, ab.shape, 1)
    g_ref[...] = _dn_gates(ab, -jnp.exp(alog_ref[...]), dtb_ref[...], lane < 2 * DN_HEADS)
    abt = _dot_nt(wt_ref[...], h)
    row = lax.broadcasted_iota(jnp.int32, abt.shape, 0)
    gt_ref[...] = _dn_gates(abt, -jnp.exp(alogc_ref[...]), dtbc_ref[...], row < 2 * DN_HEADS)


def _l1_in(h3, w, wt, alog, dtb, alogc, dtbc):
    row = lambda n: pl.BlockSpec((TM, n), lambda i: (i, 0))
    full = lambda a: pl.BlockSpec(a.shape, lambda i: (0,) * a.ndim)
    return pl.pallas_call(
        _l1_in_kernel,
        out_shape=[jax.ShapeDtypeStruct((N_TOK, DN_QKV), F32),
                   jax.ShapeDtypeStruct((N_TOK, DN_HEADS * DN_DV), F32),
                   jax.ShapeDtypeStruct((N_TOK, LANES), F32),
                   jax.ShapeDtypeStruct((N_GATE, N_TOK), F32)],
        grid=(N_TILES,),
        in_specs=[row(D_MODEL), full(w), full(wt), full(alog), full(dtb), full(alogc), full(dtbc)],
        out_specs=[row(DN_QKV), row(DN_HEADS * DN_DV), row(LANES),
                   pl.BlockSpec((N_GATE, TM), lambda i: (0, i))],
        compiler_params=_params(),
        name="l1_in_proj",
    )(h3, w, wt, alog, dtb, alogc, dtbc)


HALO = SUBLANES
SOLVE_BASE = 16


def _unit_tri_inverse(a, blk16, blk32):
    x = jnp.where(blk16, -a, 0.0)
    xb = x.astype(BF16)
    x2 = _dot(xb, xb)
    yield
    x2b = x2.astype(BF16)
    z = _dot(jnp.concatenate([x2b, xb], axis=0), x2b)
    yield
    x4 = z[:CHUNK]
    s = (x + x2) + z[CHUNK:]
    x4b = x4.astype(BF16)
    z = _dot(jnp.concatenate([x4b, s.astype(BF16)], axis=0), x4b)
    yield
    s = (s + x4) + z[CHUNK:]
    x8 = z[:CHUNK]
    z = _dot(s.astype(BF16), x8.astype(BF16))
    yield
    s = (s + x8) + z
    for off in (jnp.where(blk32 & ~blk16, a, 0.0), jnp.where(blk32, 0.0, a)):
        sb = s.astype(BF16)
        z = _dot(sb, off.astype(BF16))
        yield
        y = off + z
        z = _dot(y.astype(BF16), sb)
        yield
        s = s - (y + z)
    return s


def _dn_kernel(q_ref, k_ref, v_ref, cwq_ref, cwk_ref, cwv_ref, g_ref, gt_ref, s0_ref,
               o_ref, st_ref,
               pad_ref, wq_ref, u_ref, qk_ref, ke_ref, dl_ref, *, seq_len, heads, unroll):
    n = seq_len // CHUNK
    h0 = pl.program_id(1) * heads
    lower, upper = _tri_masks(CHUNK)
    strict_lower = lower & ~upper
    strict_upper = upper & ~lower
    lower_bf = lower.astype(BF16)
    upper_bf = upper.astype(BF16)
    ri = lax.broadcasted_iota(jnp.int32, (CHUNK, CHUNK), 0)
    ci = lax.broadcasted_iota(jnp.int32, (CHUNK, CHUNK), 1)
    blk16 = (ri // SOLVE_BASE) == (ci // SOLVE_BASE)
    blk32 = (ri // (2 * SOLVE_BASE)) == (ci // (2 * SOLVE_BASE))

    zero = jnp.zeros((HALO, DN_DK), F32)
    for t, src in enumerate((q_ref, k_ref, v_ref)):
        for hh in range(heads):
            pad_ref[t, hh, 0:HALO] = zero
            pad_ref[t, hh, HALO + seq_len:2 * HALO + seq_len] = zero
            pad_ref[t, hh, HALO:HALO + seq_len] = src[:, hh * DN_DK:(hh + 1) * DN_DK]
    o_ref[...] = jnp.zeros_like(o_ref)
    st_ref[...] = s0_ref[...]

    def conv(t, cw_ref, r0, hh):
        acc = None
        for j in range(DN_CONV):
            term = (pad_ref[t, hh, pl.ds(r0 + (HALO - DN_CONV // 2 + j), CHUNK)]
                    * cw_ref[j:j + 1, hh * DN_DK:(hh + 1) * DN_DK])
            acc = term if acc is None else acc + term
        return _silu(acc)

    def l2n(x):
        return x * lax.rsqrt(jnp.sum(x * x, axis=-1, keepdims=True) + EPS)

    lane = lax.broadcasted_iota(jnp.int32, (CHUNK, LANES), 1)

    def gate_col(gblk, col):
        return jnp.sum(jnp.where(lane == col, gblk, 0.0), axis=-1, keepdims=True)

    directions = ((lower, strict_lower, CHUNK - 1), (upper, strict_upper, 0))

    gate_row = lax.broadcasted_iota(jnp.int32, (N_GATE, CHUNK), 0)

    def prep_chain(c, hh, d, q, k, v, gblk, gcum, grow):
        mask, strict, tr = directions[d]
        beta = gate_col(gblk, h0 + hh + (2 + d) * DN_HEADS)
        gc = jnp.broadcast_to(gate_col(gcum[d], h0 + hh + d * DN_HEADS), (CHUNK, DN_DK))
        gr = jnp.sum(jnp.where(gate_row == h0 + hh + d * DN_HEADS, grow[d], 0.0), axis=0, keepdims=True)
        k_beta = k * beta
        kq = _dot_nt(jnp.concatenate([k_beta, q], axis=0).astype(BF16), k.astype(BF16))
        yield
        decay = jnp.exp(jnp.where(mask, gc[:, :CHUNK] - gr, -jnp.inf))
        a = jnp.where(strict, kq[:CHUNK] * decay, 0.0)
        qk = kq[CHUNK:] * decay
        eg = jnp.exp(gc)
        rhs = jnp.concatenate([v * beta, k_beta * eg], axis=-1)
        s = yield from _unit_tri_inverse(a, blk16, blk32)
        z = _dot(s.astype(BF16), rhs.astype(BF16))
        yield
        r = rhs + z
        tot = gc[tr:tr + 1]
        idx = (2 * hh + d) * n + c
        u_ref[idx] = r[:, :DN_DV]
        wq_ref[idx] = jnp.concatenate([r[:, DN_DV:], q * eg], axis=0).astype(BF16)
        qk_ref[idx] = qk.astype(BF16)
        ke_ref[idx] = (k * jnp.exp(tot - gc)).astype(BF16)
        dl_ref[idx] = jnp.broadcast_to(jnp.exp(tot), (SUBLANES, DN_DV))

    def prep(cg, carry):
        chains = []
        for uu in range(unroll):
            c = cg * unroll + uu
            r0 = pl.multiple_of(c * CHUNK, CHUNK)
            gblk = g_ref[pl.ds(r0, CHUNK)]
            gtblk = gt_ref[c]
            gcum = (_mask_dot(lower_bf, gblk), _mask_dot(upper_bf, gblk))
            grow = (_dot_mask(gtblk, upper_bf), _dot_mask(gtblk, lower_bf))
            for hh in range(heads):
                q = l2n(conv(0, cwq_ref, r0, hh)) * (DN_DK ** -0.5)
                k = l2n(conv(1, cwk_ref, r0, hh))
                v = conv(2, cwv_ref, r0, hh)
                for d in range(2):
                    chains.append(prep_chain(c, hh, d, q, k, v, gblk, gcum, grow))
        _interleave(chains)
        return carry

    lax.fori_loop(0, n // unroll, prep, 0)

    def scan_chain(c, hh, d):
        cc = c if d == 0 else n - 1 - c
        idx = (2 * hh + d) * n + cc
        s = st_ref[0, d, hh]
        rows = pl.ds(pl.multiple_of(cc * CHUNK, CHUNK), CHUNK)
        cols = slice(hh * DN_DV, (hh + 1) * DN_DV)
        z = _dot(wq_ref[idx], s.astype(BF16))
        yield
        v_new = (u_ref[idx] - z[:CHUNK]).astype(BF16)
        zo = _dot(qk_ref[idx], v_new)
        zs = _dot_tn(ke_ref[idx], v_new)
        yield
        o_ref[rows, cols] += z[CHUNK:] + zo
        st_ref[0, d, hh] = s * dl_ref[idx][0:1] + zs

    def scan(c, carry):
        _interleave([scan_chain(c, hh, d) for hh in range(heads) for d in range(2)])
        return carry

    lax.fori_loop(0, n, scan, 0)


def _deltanet(qkv, conv_w, gates, gates_t, s0, *, seq_len, n_seq, row_off, heads, unroll):
    off = row_off // seq_len
    n = seq_len // CHUNK
    groups = DN_HEADS // heads
    width = heads * DN_DK
    col = lambda t: pl.BlockSpec((seq_len, width), lambda s, p: (s + off, t * groups + p))
    cw = lambda t: pl.BlockSpec((DN_CONV, width), lambda s, p: (0, t * groups + p))
    st_spec = pl.BlockSpec((1, 2, heads, DN_DK, DN_DV), lambda s, p: (s, 0, p, 0, 0))
    slots = 2 * heads * n
    return pl.pallas_call(
        functools.partial(_dn_kernel, seq_len=seq_len, heads=heads, unroll=unroll),
        out_shape=[jax.ShapeDtypeStruct((n_seq * seq_len, DN_HEADS * DN_DV), F32),
                   jax.ShapeDtypeStruct((n_seq, 2, DN_HEADS, DN_DK, DN_DV), F32)],
        grid=(n_seq, groups),
        in_specs=[col(0), col(1), col(2), cw(0), cw(1), cw(2),
                  pl.BlockSpec((seq_len, LANES), lambda s, p: (s + off, 0)),
                  pl.BlockSpec((n, N_GATE, CHUNK), lambda s, p: (s + off, 0, 0)),
                  st_spec],
        out_specs=[pl.BlockSpec((seq_len, width), lambda s, p: (s, p)), st_spec],
        scratch_shapes=[pltpu.VMEM((3, heads, seq_len + 2 * HALO, DN_DK), F32),
                        pltpu.VMEM((slots, 2 * CHUNK, DN_DK), BF16),
                        pltpu.VMEM((slots, CHUNK, DN_DV), F32),
                        pltpu.VMEM((slots, CHUNK, CHUNK), BF16),
                        pltpu.VMEM((slots, CHUNK, DN_DK), BF16),
                        pltpu.VMEM((slots, SUBLANES, DN_DV), F32)],
        compiler_params=_params(),
        name=f"deltanet_{seq_len}",
    )(qkv, qkv, qkv, conv_w, conv_w, conv_w, gates, gates_t, s0)


def _l1_out_kernel(oc_ref, ol_ref, z_ref, on_ref, w_ref, x_ref, mod_ref, n2_ref, rh_ref, rl_ref,
                   x3_ref, h4_ref, route_ref):
    m = mod_ref[0]
    on = on_ref[...]
    is_ctx = pl.program_id(0) < CTX_TILES

    def half(first_row, n_rows):
        rows = slice(first_row, first_row + n_rows)
        parts = []
        for hd in range(DN_HEADS):
            cs = slice(hd * DN_DV, (hd + 1) * DN_DV)
            o = jnp.where(is_ctx, oc_ref[rows, cs], ol_ref[rows, cs])
            parts.append((_rms(o, on) * _silu(z_ref[rows, cs])).astype(BF16))
        y = _dot(jnp.concatenate(parts, axis=-1), w_ref[...])
        yield
        x3 = x_ref[rows, :] + m[2:3] * y
        x3_ref[rows, :] = x3
        h4 = _rms(x3, n2_ref[...]) * (1.0 + m[4:5]) + m[3:4]
        for s in range(ROW_TILES):
            h4_ref[pl.ds(first_row * ROW_TILES + s, n_rows, stride=ROW_TILES), :] = h4[:, s * LANES:(s + 1) * LANES]
        h4_hi = h4.astype(BF16)
        h4_lo = (h4 - h4_hi.astype(F32)).astype(BF16)
        logits = (_dot(h4_hi, rh_ref[...]) + _dot(h4_hi, rl_ref[...])) + _dot(h4_lo, rh_ref[...])
        yield
        lane = lax.broadcasted_iota(jnp.int32, logits.shape, 1).astype(F32)
        logits = jnp.where(lane < N_EXPERTS, logits, -jnp.inf)
        m1 = jnp.max(logits, axis=-1, keepdims=True)
        i1 = jnp.min(jnp.where(logits == m1, lane, float(LANES)), axis=-1, keepdims=True)
        rest = jnp.where(lane == i1, -jnp.inf, logits)
        m2 = jnp.max(rest, axis=-1, keepdims=True)
        i2 = jnp.min(jnp.where(rest == m2, lane, float(LANES)), axis=-1, keepdims=True)
        e2 = jnp.exp(m2 - m1)
        g1 = 1.0 / (1.0 + e2)
        g2 = e2 / (1.0 + e2)
        route_ref[rows, :] = jnp.where(lane == 0, i1, jnp.where(lane == 1, i2, jnp.where(
            lane == 2, g1, jnp.where(lane == 3, g2, 0.0))))

    n_rows = x_ref.shape[0] // 2
    _interleave([half(0, n_rows), half(n_rows, n_rows)])


def _l1_out(o_dn_c, o_dn_l, z, onorm, w, x2, mod, norm2, r_hi, r_lo):
    row = lambda n: pl.BlockSpec((TM, n), lambda i: (i, 0))
    full = lambda a: pl.BlockSpec(a.shape, lambda i: (0,) * a.ndim)
    return pl.pallas_call(
        _l1_out_kernel,
        out_shape=[jax.ShapeDtypeStruct((N_TOK, D_MODEL), F32),
                   jax.ShapeDtypeStruct((N_TOK * ROW_TILES, LANES), F32),
                   jax.ShapeDtypeStruct((N_TOK, LANES), F32)],
        grid=(N_TILES,),
        in_specs=[_ctx_rows(D_MODEL), _lat_rows(D_MODEL), row(D_MODEL), full(onorm), full(w),
                  row(D_MODEL), pl.BlockSpec((1, N_MOD, D_MODEL), lambda i: (_mod_index(i), 0, 0)),
                  full(norm2), full(r_hi), full(r_lo)],
        out_specs=[row(D_MODEL), pl.BlockSpec((TM * ROW_TILES, LANES), lambda i: (i, 0)), row(LANES)],
        compiler_params=_params(),
        name="l1_out_proj_router",
    )(o_dn_c, o_dn_l, z, onorm, w, x2, mod, norm2, r_hi, r_lo)


ROW_DMA_UNROLL = 8


def _moe_kernel(te_ref, nt_ref, nv_ref, src_ref, dst_ref, h_hbm, w1_ref, w3_ref, w2_ref, y_hbm,
                xs_ref, xb_ref, acc_ref, yb_ref, gsem, ssem):
    i = pl.program_id(0)
    j = pl.program_id(1)
    n_used = nt_ref[0]
    slot = i % 2

    def for_valid_rows(tile, fn):
        nv = nv_ref[tile]
        groups = nv // ROW_DMA_UNROLL

        def group(g, carry):
            for u in range(ROW_DMA_UNROLL):
                fn(g * ROW_DMA_UNROLL + u)
            return carry

        def single(r, carry):
            fn(r)
            return carry

        lax.fori_loop(0, groups, group, 0)
        lax.fori_loop(groups * ROW_DMA_UNROLL, nv, single, 0)

    def row_tile(n):
        return pl.ds(pl.multiple_of(n * ROW_TILES, ROW_TILES), ROW_TILES)

    def gather(tile, sl, start):
        def row(r):
            cp = pltpu.make_async_copy(h_hbm.at[row_tile(src_ref[tile * MOE_TM + r])],
                                       xs_ref.at[sl, row_tile(r)], gsem.at[sl])
            cp.start() if start else cp.wait()
        for_valid_rows(tile, row)

    def scatter(tile, sl, start):
        def row(r):
            cp = pltpu.make_async_copy(yb_ref.at[sl, row_tile(r)],
                                       y_hbm.at[row_tile(dst_ref[tile * MOE_TM + r])], ssem.at[sl])
            cp.start() if start else cp.wait()
        for_valid_rows(tile, row)

    @pl.when((i < n_used) & (j == 0))
    def _():
        @pl.when(i == 0)
        def _():
            xs_ref[...] = jnp.zeros_like(xs_ref)
            gather(i, slot, True)
        gather(i, slot, False)

        @pl.when(i + 1 < n_used)
        def _():
            gather(i + 1, 1 - slot, True)
        xb_ref[...] = _load_row_tiles(xs_ref, MOE_TM, (slot,)).astype(BF16)
        acc_ref[...] = jnp.zeros_like(acc_ref)

    @pl.when(i < n_used)
    def _():
        x = xb_ref[...]
        g = (_silu(_dot(x, w1_ref[0])) * _dot(x, w3_ref[0])).astype(BF16)
        acc_ref[...] += _dot(g, w2_ref[0])

    @pl.when((i < n_used) & (j == pl.num_programs(1) - 1))
    def _():
        @pl.when(i >= 2)
        def _():
            scatter(i - 2, slot, False)
        _store_row_tiles(yb_ref, acc_ref[...], (slot,))
        scatter(i, slot, True)

    @pl.when((i == pl.num_programs(0) - 1) & (j == pl.num_programs(1) - 1))
    def _():
        @pl.when(n_used >= 2)
        def _():
            scatter(n_used - 2, n_used % 2, False)
        scatter(n_used - 1, (n_used - 1) % 2, False)


def _moe(tile_expert, n_used, n_valid, src_rows, dst_rows, h4, w1, w3, w2):
    def wmap(i, j, te, nt, nv, src, dst):
        return (te[i], 0, jnp.where(i < nt[0], j, MOE_FF_STEPS - 1))

    def w2map(i, j, te, nt, nv, src, dst):
        return (te[i], jnp.where(i < nt[0], j, MOE_FF_STEPS - 1), 0)

    grid_spec = pltpu.PrefetchScalarGridSpec(
        num_scalar_prefetch=5,
        grid=(MOE_TILES, MOE_FF_STEPS),
        in_specs=[pl.BlockSpec(memory_space=pl.ANY),
                  pl.BlockSpec((1, D_MODEL, MOE_FF), wmap),
                  pl.BlockSpec((1, D_MODEL, MOE_FF), wmap),
                  pl.BlockSpec((1, MOE_FF, D_MODEL), w2map)],
        out_specs=pl.BlockSpec(memory_space=pl.ANY),
        scratch_shapes=[pltpu.VMEM((2, MOE_TM * ROW_TILES, LANES), F32),
                        pltpu.VMEM((MOE_TM, D_MODEL), BF16),
                        pltpu.VMEM((MOE_TM, D_MODEL), F32),
                        pltpu.VMEM((2, MOE_TM * ROW_TILES, LANES), F32),
                        pltpu.SemaphoreType.DMA((2,)),
                        pltpu.SemaphoreType.DMA((2,))])
    return pl.pallas_call(
        _moe_kernel,
        out_shape=jax.ShapeDtypeStruct((2 * N_TOK * ROW_TILES, LANES), F32),
        grid_spec=grid_spec,
        compiler_params=_params(has_side_effects=True),
        name="routed_experts",
    )(tile_expert, n_used, n_valid, src_rows, dst_rows, h4, w1, w3, w2)


def _combine_kernel(y0_ref, y1_ref, route_ref, x_ref, mod_ref, fn_ref, o_ref):
    m = mod_ref[0]
    route = route_ref[...]
    y = (route[:, 2:3] * _load_row_tiles(y0_ref, TM)) + (route[:, 3:4] * _load_row_tiles(y1_ref, TM))
    o_ref[...] = _rms(x_ref[...] + m[5:6] * y, fn_ref[...])


def _combine(y, route, x3, mod, final_norm, *, first_tile, n_tiles):
    row = lambda n: pl.BlockSpec((TM, n), lambda i: (i + first_tile, 0))
    slot = lambda k: pl.BlockSpec((TM * ROW_TILES, LANES), lambda i: (i + first_tile + k * N_TILES, 0))
    return pl.pallas_call(
        _combine_kernel,
        out_shape=jax.ShapeDtypeStruct((n_tiles * TM, D_MODEL), F32),
        grid=(n_tiles,),
        in_specs=[slot(0), slot(1), row(LANES), row(D_MODEL),
                  pl.BlockSpec((1, N_MOD, D_MODEL), lambda i: (_mod_index(i + first_tile), 0, 0)),
                  pl.BlockSpec((1, D_MODEL), lambda i: (0, 0))],
        out_specs=pl.BlockSpec((TM, D_MODEL), lambda i: (i, 0)),
        compiler_params=_params(),
        name="combine_final_norm",
    )(y, y, route, x3, mod, final_norm)


def _rope_tables():
    rows = LAT_LEN // GRID_W
    row = jnp.repeat(jnp.arange(rows), GRID_W).astype(F32)
    col = (jnp.arange(rows * GRID_W) % GRID_W).astype(F32)
    half = SWA_HD // 2
    inv_freq = ROPE_BASE ** (-jnp.arange(0, half, 2, dtype=F32) / half)
    ang_r = row[:, None] * inv_freq
    ang_c = col[:, None] * inv_freq
    cos = jnp.concatenate([jnp.cos(ang_r)] * 2 + [jnp.cos(ang_c)] * 2, axis=-1)
    sin = jnp.concatenate([-jnp.sin(ang_r), jnp.sin(ang_r), -jnp.sin(ang_c), jnp.sin(ang_c)], axis=-1)
    cos = jnp.tile(cos, (1, SWA_HEADS))
    sin = jnp.tile(sin, (1, SWA_HEADS))
    cos = jnp.concatenate([cos, jnp.ones((TM, cos.shape[1]), F32)], axis=0)
    sin = jnp.concatenate([sin, jnp.zeros((TM, sin.shape[1]), F32)], axis=0)
    return cos, sin


def _dispatch(route):
    e = jnp.concatenate([route[:, 0], route[:, 1]]).astype(jnp.int32)
    onehot = (e[:, None] == jnp.arange(N_EXPERTS)[None, :]).astype(jnp.int32)
    counts = jnp.sum(onehot, axis=0)
    rank = jnp.sum((jnp.cumsum(onehot, axis=0) - onehot) * onehot, axis=1)
    padded = ((counts + MOE_TM - 1) // MOE_TM) * MOE_TM
    ends = jnp.cumsum(padded)
    starts = ends - padded
    pos = starts[e] + rank
    dst = jnp.zeros((MOE_ROWS,), jnp.int32).at[pos].set(jnp.arange(2 * N_TOK, dtype=jnp.int32))
    src = dst % N_TOK
    n_used = (ends[-1] // MOE_TM).astype(jnp.int32)
    tile_start = jnp.arange(MOE_TILES, dtype=jnp.int32) * MOE_TM
    tile_expert = jnp.minimum(jnp.sum(tile_start[:, None] >= ends[None, :], axis=1), N_EXPERTS - 1)
    n_valid = jnp.clip((starts + counts)[tile_expert] - tile_start, 0, MOE_TM).astype(jnp.int32)
    last = jnp.take(tile_expert, jnp.maximum(n_used - 1, 0))
    tile_expert = jnp.where(jnp.arange(MOE_TILES) < n_used, tile_expert, last).astype(jnp.int32)
    return tile_expert, n_used.reshape(1), n_valid, src, dst


def kernel(x_prompt, x_sample, c, state_gla_l0, cache_k_l0, cache_v_l0, state_dn_l1, c_ctx, final_norm, ada_w_l0, ada_b_l0, norm1_l0, norm2_l0, in_w_l0, gla_w2_l0, gla_b_l0, gla_onorm_l0, sink_l0, out_w_l0, ffn_w1_l0, ffn_w3_l0, ffn_w2_l0, ada_w_l1, ada_b_l1, norm1_l1, norm2_l1, in_w_l1, conv_w_l1, a_log_l1, dt_bias_l1, dn_onorm_l1, out_w_l1, router_l1, moe_w1_l1, moe_w3_l1, moe_w2_l1):
    D = D_MODEL
    x_ctx = x_prompt.reshape(N_CTX_TOK, D)
    x_lat = x_sample.reshape(N_LAT_TOK, D)
    row2 = lambda v: v.reshape(1, -1)

    cond = jnp.concatenate([c_ctx[None], c, jnp.zeros((2 * SUBLANES - 1 - N_LAT_SEQ, D), F32)], axis=0)
    mod0 = _modulation(cond, ada_w_l0, ada_b_l0).reshape(-1, N_MOD, D)
    mod1 = _modulation(cond, ada_w_l1, ada_b_l1).reshape(-1, N_MOD, D)

    gq_w, gk_w, gv_w, gr_w, glrf_w, glrb_w, sq_w, sk_w, sv_w = jnp.split(
        in_w_l0, [256, 512, 1024, 1536, 1552, 1568, 2080, 2208], axis=1)
    w0 = jnp.concatenate([gq_w, gk_w, gv_w, gr_w, sq_w, sk_w, sv_w, glrf_w, glrb_w,
                          jnp.zeros((D, L0_END - L0_GLR - 2 * GLA_RANK), F32)], axis=1).astype(BF16)
    w2f = jnp.zeros((LANES, GLA_HEADS * GLA_DK), F32).at[:GLA_RANK].set(gla_w2_l0[0]).astype(BF16)
    w2b = jnp.zeros((LANES, GLA_HEADS * GLA_DK), F32).at[GLA_RANK:2 * GLA_RANK].set(gla_w2_l0[1]).astype(BF16)
    cos, sin = _rope_tables()
    gq, gk, gv, gr, laf, lab, sq, sk, sv = _l0_in(x_ctx, x_lat, mod0, row2(norm1_l0), w0, w2f, w2b,
                                                  gla_b_l0, cos, sin)

    onorm0 = row2(gla_onorm_l0)
    zero_state = jnp.zeros((N_CTX_SEQ, 2, GLA_HEADS, GLA_DV, GLA_DK), F32)
    o_gla_c, st_gla = _gla(gq, gk, gv, gr, laf, lab, zero_state, onorm0,
                           seq_len=CTX_LEN, n_seq=N_CTX_SEQ, row_off=0)
    o_gla_l, _ = _gla(gq, gk, gv, gr, laf, lab, jnp.swapaxes(state_gla_l0, -1, -2), onorm0,
                      seq_len=LAT_LEN, n_seq=N_LAT_SEQ, row_off=N_CTX_TOK)
    o_swa_c = _swa_ctx(sink_l0, sq, sk, sv)
    o_swa_l = _swa_lat(sink_l0, sq, sk, sv,
                       cache_k_l0.reshape(N_LAT_SEQ, -1, SWA_KV_HEADS * SWA_HD),
                       cache_v_l0.reshape(N_LAT_SEQ, -1, SWA_KV_HEADS * SWA_HD))
    x1, h2 = _l0_out(o_gla_c, o_gla_l, o_swa_c, o_swa_l, out_w_l0.astype(BF16), x_ctx, x_lat, mod0,
                     row2(norm2_l0))

    x2, h3 = _ffn(h2, x1, ffn_w1_l0.astype(BF16), ffn_w3_l0.astype(BF16), ffn_w2_l0.astype(BF16),
                  mod0, mod1, row2(norm1_l1))

    w_ab = in_w_l1[:, L1_AB:]
    wl1 = jnp.concatenate([in_w_l1, jnp.zeros((D, L1_END - in_w_l1.shape[1]), F32)], axis=1).astype(BF16)
    zeros_g = jnp.zeros((2 * DN_HEADS,), F32)
    alog = jnp.concatenate([a_log_l1.reshape(-1), zeros_g])
    dtb = jnp.concatenate([dt_bias_l1.reshape(-1), zeros_g])
    pad_lane = lambda v: jnp.concatenate([v, jnp.zeros((LANES - N_GATE,), F32)]).reshape(1, LANES)
    qkv, z, gates, gates_t = _l1_in(h3, wl1, w_ab.T.astype(BF16), pad_lane(alog), pad_lane(dtb),
                                    alog.reshape(N_GATE, 1), dtb.reshape(N_GATE, 1))
    gates_t = gates_t.reshape(N_GATE, N_TOK // CHUNK, CHUNK).transpose(1, 0, 2)
    zero_dn = jnp.zeros((N_CTX_SEQ, 2, DN_HEADS, DN_DK, DN_DV), F32)
    o_dn_c, st_dn = _deltanet(qkv, conv_w_l1, gates, gates_t, zero_dn,
                              seq_len=CTX_LEN, n_seq=N_CTX_SEQ, row_off=0, heads=4, unroll=4)
    o_dn_l, _ = _deltanet(qkv, conv_w_l1, gates, gates_t, state_dn_l1,
                          seq_len=LAT_LEN, n_seq=N_LAT_SEQ, row_off=N_CTX_TOK, heads=2, unroll=8)
    router = jnp.concatenate([router_l1, jnp.zeros((D, LANES - N_EXPERTS), F32)], axis=1)
    r_hi = router.astype(BF16)
    r_lo = (router - r_hi.astype(F32)).astype(BF16)
    x3, h4, route = _l1_out(o_dn_c, o_dn_l, z, row2(dn_onorm_l1), out_w_l1.astype(BF16), x2, mod1,
                            row2(norm2_l1), r_hi, r_lo)

    tile_expert, n_used, n_valid, src, dst = _dispatch(route)
    y = _moe(tile_expert, n_used, n_valid, src, dst, h4,
             moe_w1_l1.astype(BF16), moe_w3_l1.astype(BF16), moe_w2_l1.astype(BF16))
    fn = row2(final_norm)
    y_prompt = _combine(y, route, x3, mod1, fn, first_tile=0, n_tiles=CTX_TILES)
    y_sample = _combine(y, route, x3, mod1, fn, first_tile=CTX_TILES, n_tiles=N_TILES - CTX_TILES)
    y_prompt = y_prompt.reshape(N_CTX_SEQ, CTX_LEN, D)
    y_sample = y_sample.reshape(N_LAT_SEQ, LAT_LEN, D)
    new_state_gla = jnp.swapaxes(st_gla, -1, -2)
    new_k = sk[:N_CTX_TOK].reshape(N_CTX_SEQ, CTX_LEN, SWA_KV_HEADS, SWA_HD)
    new_v = sv[:N_CTX_TOK].reshape(N_CTX_SEQ, CTX_LEN, SWA_KV_HEADS, SWA_HD)
    return (y_prompt, y_sample, new_state_gla, new_k, new_v, st_dn)
```

```python
import functools
import math

import jax
import jax.numpy as jnp
from jax import lax
from jax.experimental import pallas as pl
from jax.experimental.pallas import tpu as pltpu

F32 = jnp.float32
BF16 = jnp.bfloat16

D_MODEL = 1024
N_CTX_SEQ = 16
CTX_LEN = 256
N_LAT_SEQ = 8
LAT_LEN = 2048
N_CTX_TOK = N_CTX_SEQ * CTX_LEN
N_LAT_TOK = N_LAT_SEQ * LAT_LEN
N_TOK = N_CTX_TOK + N_LAT_TOK
GRID_W = 64
EPS = 1e-6
N_MOD = 6

GLA_HEADS = 4
GLA_DK = 64
GLA_DV = 128
GLA_RANK = 16
GLA_TAU = 16.0
CHUNK = 64
GLA_UNROLL = 8

SWA_HEADS = 8
SWA_KV_HEADS = 2
SWA_GROUP = SWA_HEADS // SWA_KV_HEADS
SWA_HD = 64
SWA_BLOCK = 128
ROPE_BASE = 10000.0

DN_HEADS = 8
DN_DK = 128
DN_DV = 128
DN_CONV = 5
DN_QKV = 3 * DN_HEADS * DN_DK

D_FF_DENSE = 2816
N_EXPERTS = 8
D_FF_EXPERT = 3584

LANES = 128
SUBLANES = 8
MXU_DIM = 256
VMEM_LIMIT = 56 * 1024 * 1024

TM = 512
N_TILES = N_TOK // TM
CTX_TILES = N_CTX_TOK // TM
LAT_TILES_PER_SEQ = LAT_LEN // TM
FF_CHUNK = MXU_DIM
MOE_TM = 512
MOE_FF = 1792
MOE_ROWS = 2 * N_TOK + N_EXPERTS * MOE_TM
MOE_TILES = MOE_ROWS // MOE_TM
MOE_FF_STEPS = D_FF_EXPERT // MOE_FF


def _params(**kw):
    return pltpu.CompilerParams(vmem_limit_bytes=VMEM_LIMIT, **kw)


def _sigmoid(x):
    return 1.0 / (1.0 + jnp.exp(-x))


def _silu(x):
    return x * _sigmoid(x)


def _softplus(x):
    return jnp.maximum(x, 0.0) + jnp.log(1.0 + jnp.exp(-jnp.abs(x)))


def _rms(x, g):
    return x * lax.rsqrt(jnp.mean(x * x, axis=-1, keepdims=True) + EPS) * g


def _dot(a, b):
    return jnp.dot(a, b, preferred_element_type=F32)


def _dot_nt(a, b):
    return lax.dot_general(a, b, (((1,), (1,)), ((), ())), preferred_element_type=F32)


def _dot_tn(a, b):
    return lax.dot_general(a, b, (((0,), (0,)), ((), ())), preferred_element_type=F32)


def _split3(x):
    hi = x.astype(BF16)
    r = x - hi.astype(F32)
    mid = r.astype(BF16)
    lo = (r - mid.astype(F32)).astype(BF16)
    return hi, mid, lo


def _mask_dot(mask_bf, x):
    hi, mid, lo = _split3(x)
    return (_dot(mask_bf, hi) + _dot(mask_bf, mid)) + _dot(mask_bf, lo)


def _dot_mask(x, mask_bf):
    hi, mid, lo = _split3(x)
    return (_dot(hi, mask_bf) + _dot(mid, mask_bf)) + _dot(lo, mask_bf)


def _interleave(chains):
    results = [None] * len(chains)
    active = list(enumerate(chains))
    while active:
        still = []
        for i, chain in active:
            try:
                next(chain)
                still.append((i, chain))
            except StopIteration as stop:
                results[i] = stop.value
        active = still
    return results


def _tri_masks(n):
    r = lax.broadcasted_iota(jnp.int32, (n, n), 0)
    c = lax.broadcasted_iota(jnp.int32, (n, n), 1)
    return c <= r, c >= r


def _ctx_rows(w):
    return pl.BlockSpec((TM, w), lambda i: (jnp.minimum(i, CTX_TILES - 1), 0))


def _lat_rows(w):
    return pl.BlockSpec((TM, w), lambda i: (jnp.maximum(i - CTX_TILES, 0), 0))


def _pick_rows(ctx_ref, lat_ref, cols=slice(None)):
    return jnp.where(pl.program_id(0) < CTX_TILES, ctx_ref[:, cols], lat_ref[:, cols])


ROW_TILES = D_MODEL // LANES


def _store_row_tiles(ref, x, lead=()):
    rows = x.shape[0]
    for s in range(ROW_TILES):
        ref[lead + (pl.ds(s, rows, stride=ROW_TILES), slice(None))] = x[:, s * LANES:(s + 1) * LANES]


def _load_row_tiles(ref, rows, lead=()):
    return jnp.concatenate([ref[lead + (pl.ds(s, rows, stride=ROW_TILES), slice(None))]
                            for s in range(ROW_TILES)], axis=-1)


def _mod_index(i):
    return jnp.where(i < CTX_TILES, 0, 1 + (i - CTX_TILES) // LAT_TILES_PER_SEQ)


def _mod_kernel(c_ref, w_ref, b_ref, o_ref):
    s = _silu(c_ref[...]).astype(BF16)
    o_ref[...] = _dot(s, w_ref[...].astype(BF16)) + b_ref[...]


def _modulation(cond, w, b):
    m, d = cond.shape
    n = w.shape[1]
    tn = n // 4
    return pl.pallas_call(
        _mod_kernel,
        out_shape=jax.ShapeDtypeStruct((m, n), F32),
        grid=(n // tn,),
        in_specs=[pl.BlockSpec((m, d), lambda j: (0, 0)),
                  pl.BlockSpec((d, tn), lambda j: (0, j)),
                  pl.BlockSpec((1, tn), lambda j: (0, j))],
        out_specs=pl.BlockSpec((m, tn), lambda j: (0, j)),
        compiler_params=_params(),
        name="modulation",
    )(cond, w, b.reshape(1, n))


L0_GQ, L0_GK, L0_GV, L0_GR, L0_SQ, L0_SK, L0_SV, L0_GLR, L0_END = (
    0, 256, 512, 1024, 1536, 2048, 2176, 2304, 2432)


def _rope(x, cos, sin):
    n = x.shape[-1]
    lane = lax.broadcasted_iota(jnp.int32, x.shape, 1)
    first = (lane % 32) < 16
    partner = jnp.where(first, pltpu.roll(x, n - 16, 1), pltpu.roll(x, 16, 1))
    return x * cos + partner * sin


def _log_sigmoid(x):
    return jnp.minimum(x, 0.0) - jnp.log(1.0 + jnp.exp(-jnp.abs(x)))


def _l0_in_kernel(xc_ref, xl_ref, mod_ref, n1_ref, w_ref, w2f_ref, w2b_ref, gb_ref, cos_ref, sin_ref,
                  gq_ref, gk_ref, gv_ref, gr_ref, laf_ref, lab_ref, sq_ref, sk_ref, sv_ref):
    m = mod_ref[0]
    h = _rms(_pick_rows(xc_ref, xl_ref), n1_ref[...]) * (1.0 + m[1:2]) + m[0:1]
    hb = h.astype(BF16)

    def proj(a, b):
        return _dot(hb, w_ref[:, a:b])

    gq_ref[...] = proj(L0_GQ, L0_GK) * (GLA_DK ** -0.5)
    gk_ref[...] = proj(L0_GK, L0_GV)
    gv_ref[...] = proj(L0_GV, L0_GR)
    gr_ref[...] = proj(L0_GR, L0_SQ)
    cos = cos_ref[...]
    sin = sin_ref[...]
    sq_ref[...] = _rope(proj(L0_SQ, L0_SK), cos, sin)
    nk = L0_SV - L0_SK
    sk_ref[...] = _rope(proj(L0_SK, L0_SV), cos[:, :nk], sin[:, :nk])
    sv_ref[...] = proj(L0_SV, L0_GLR)
    glr = proj(L0_GLR, L0_END).astype(BF16)
    gb = gb_ref[...]
    laf_ref[...] = _log_sigmoid(_dot(glr, w2f_ref[...]) + gb[0:1]) * (1.0 / GLA_TAU)
    lab_ref[...] = _log_sigmoid(_dot(glr, w2b_ref[...]) + gb[1:2]) * (1.0 / GLA_TAU)


def _rope_index(i):
    return jnp.where(i < CTX_TILES, LAT_TILES_PER_SEQ, (i - CTX_TILES) % LAT_TILES_PER_SEQ)


def _l0_in(x_ctx, x_lat, mod, norm1, w, w2f, w2b, gb, cos, sin):
    row = lambda n: pl.BlockSpec((TM, n), lambda i: (i, 0))
    full = lambda a: pl.BlockSpec(a.shape, lambda i: (0,) * a.ndim)
    widths = (256, 256, 512, 512, 256, 256, 512, 128, 128)
    return pl.pallas_call(
        _l0_in_kernel,
        out_shape=[jax.ShapeDtypeStruct((N_TOK, n), F32) for n in widths],
        grid=(N_TILES,),
        in_specs=[_ctx_rows(D_MODEL), _lat_rows(D_MODEL),
                  pl.BlockSpec((1, N_MOD, D_MODEL), lambda i: (_mod_index(i), 0, 0)),
                  full(norm1), full(w), full(w2f), full(w2b), full(gb),
                  pl.BlockSpec((TM, 512), lambda i: (_rope_index(i), 0)),
                  pl.BlockSpec((TM, 512), lambda i: (_rope_index(i), 0))],
        out_specs=[row(n) for n in widths],
        compiler_params=_params(),
        name="l0_in_proj",
    )(x_ctx, x_lat, mod, norm1, w, w2f, w2b, gb, cos, sin)


def _gla_kernel(q_ref, k_ref, v_ref, r_ref, laf_ref, lab_ref, s0_ref, on_ref,
                o_ref, st_ref, oacc_ref, *, seq_len):
    n = seq_len // CHUNK
    lower, upper = _tri_masks(CHUNK)
    lower_bf = lower.astype(BF16)
    upper_bf = upper.astype(BF16)
    oacc_ref[...] = jnp.zeros_like(oacc_ref)
    st_ref[...] = s0_ref[...]

    def chain(d, chunks, la_ref, mask_bf, mask, tot_row):
        rows = [pl.ds(pl.multiple_of(c * CHUNK, CHUNK), CHUNK) for c in chunks]
        cums = [_mask_dot(mask_bf, la_ref[r, :]) for r in rows]
        yield
        parts = []
        for r, b2 in zip(rows, cums):
            for hh in range(2):
                ck = slice(hh * GLA_DK, (hh + 1) * GLA_DK)
                cv = slice(hh * GLA_DV, (hh + 1) * GLA_DV)
                b = b2[:, ck]
                tot = b[tot_row:tot_row + 1]
                q = q_ref[r, ck]
                k = k_ref[r, ck]
                q_dec = (q * jnp.exp(b)).astype(BF16)
                k_inv = (k * jnp.exp(-b)).astype(BF16)
                k_end = (k * jnp.exp(tot - b)).astype(BF16)
                vb = v_ref[r, cv].astype(BF16)
                parts.append((r, hh, cv, jnp.exp(tot), vb, q_dec, _dot_nt(q_dec, k_inv), _dot_tn(vb, k_end)))
        yield
        states = [st_ref[0, d, hh] for hh in range(2)]
        outs = []
        for r, hh, cv, decay, vb, q_dec, attn, kv in parts:
            o_intra = _dot(jnp.where(mask, attn, 0.0).astype(BF16), vb)
            o_inter = _dot_nt(q_dec, states[hh].astype(BF16))
            states[hh] = states[hh] * decay + kv
            outs.append((r, cv, o_intra, o_inter))
        yield
        for r, cv, o_intra, o_inter in outs:
            oacc_ref[r, cv] += o_intra + o_inter
        for hh in range(2):
            st_ref[0, d, hh] = states[hh]

    unroll = min(GLA_UNROLL, n)

    def body(g, carry):
        fwd = [g * unroll + t for t in range(unroll)]
        bwd = [n - 1 - c for c in fwd]
        _interleave([chain(0, fwd, laf_ref, lower_bf, lower, CHUNK - 1),
                     chain(1, bwd, lab_ref, upper_bf, upper, 0)])
        return carry

    lax.fori_loop(0, n // unroll, body, 0)
    for hh in range(2):
        cv = slice(hh * GLA_DV, (hh + 1) * GLA_DV)
        o_ref[:, cv] = _rms(oacc_ref[:, cv], on_ref[...]) * _silu(r_ref[:, cv])


def _gla(gq, gk, gv, gr, laf, lab, s0t, onorm, *, seq_len, n_seq, row_off):
    off = row_off // seq_len
    blk = lambda w: pl.BlockSpec((seq_len, w), lambda s, p: (s + off, p))
    st_spec = pl.BlockSpec((1, 2, 2, GLA_DV, GLA_DK), lambda s, p: (s, 0, p, 0, 0))
    return pl.pallas_call(
        functools.partial(_gla_kernel, seq_len=seq_len),
        out_shape=[jax.ShapeDtypeStruct((n_seq * seq_len, GLA_HEADS * GLA_DV), F32),
                   jax.ShapeDtypeStruct((n_seq, 2, GLA_HEADS, GLA_DV, GLA_DK), F32)],
        grid=(n_seq, GLA_HEADS // 2),
        in_specs=[blk(2 * GLA_DK), blk(2 * GLA_DK), blk(2 * GLA_DV), blk(2 * GLA_DV),
                  blk(2 * GLA_DK), blk(2 * GLA_DK), st_spec,
                  pl.BlockSpec((1, GLA_DV), lambda s, p: (0, 0))],
        out_specs=[pl.BlockSpec((seq_len, 2 * GLA_DV), lambda s, p: (s, p)), st_spec],
        scratch_shapes=[pltpu.VMEM((seq_len, 2 * GLA_DV), F32)],
        compiler_params=_params(),
        name=f"gla_{seq_len}",
    )(gq, gk, gv, gr, laf, lab, s0t, onorm)


def _attend(q, kcat, vcat, mask, sink_ref):
    def head(h):
        j = h // SWA_GROUP
        cj = slice(j * SWA_HD, (j + 1) * SWA_HD)
        s = _dot_nt(q[:, h * SWA_HD:(h + 1) * SWA_HD].astype(BF16), kcat[:, cj])
        yield
        if mask is not None:
            s = jnp.where(mask, s, -jnp.inf)
        sink = sink_ref[h]
        m = jnp.maximum(jnp.max(s, axis=-1, keepdims=True), sink)
        p = jnp.exp(s - m)
        denom = jnp.sum(p, axis=-1, keepdims=True) + jnp.exp(sink - m)
        o = _dot(p.astype(BF16), vcat[:, cj])
        yield
        return o / denom

    return jnp.concatenate(_interleave([head(h) for h in range(SWA_HEADS)]), axis=-1)


def _swa_ctx_kernel(sink_ref, q_ref, k_ref, v_ref, o_ref):
    o_ref[...] = _attend(q_ref[...] * (SWA_HD ** -0.5), k_ref[...].astype(BF16),
                         v_ref[...].astype(BF16), None, sink_ref)


def _swa_ctx(sink, sq, sk, sv):
    blk = lambda w: pl.BlockSpec((CTX_LEN, w), lambda b: (b, 0))
    return pl.pallas_call(
        _swa_ctx_kernel,
        out_shape=jax.ShapeDtypeStruct((N_CTX_TOK, SWA_HEADS * SWA_HD), F32),
        grid=(N_CTX_SEQ,),
        in_specs=[pl.BlockSpec(memory_space=pltpu.SMEM), blk(512), blk(128), blk(128)],
        out_specs=blk(512),
        compiler_params=_params(),
        name="swa_ctx",
    )(sink, sq, sk, sv)


def _swa_lat_kernel(sink_ref, q_ref, kc_ref, vc_ref, kp_ref, kn_ref, k_ref, vp_ref, vn_ref, v_ref,
                    o_ref):
    i = pl.program_id(1)
    nb = pl.num_programs(1)
    kcat = jnp.concatenate([kc_ref[0], kp_ref[...], k_ref[...], kn_ref[...]], axis=0).astype(BF16)
    vcat = jnp.concatenate([vc_ref[0], vp_ref[...], v_ref[...], vn_ref[...]], axis=0).astype(BF16)
    n_ctx = kc_ref.shape[1]
    n = n_ctx + 3 * SWA_BLOCK
    r = lax.broadcasted_iota(jnp.int32, (SWA_BLOCK, n), 0)
    c = lax.broadcasted_iota(jnp.int32, (SWA_BLOCK, n), 1) - n_ctx
    key_pos = (i - 1) * SWA_BLOCK + c
    dist = SWA_BLOCK + r - c
    mask = (c < 0) | ((jnp.abs(dist) <= SWA_BLOCK) & (key_pos >= 0) & (key_pos < nb * SWA_BLOCK))
    o_ref[...] = _attend(q_ref[...] * (SWA_HD ** -0.5), kcat, vcat, mask, sink_ref)


def _swa_lat(sink, sq, sk, sv, cache_k, cache_v):
    nb = LAT_LEN // SWA_BLOCK
    off = N_CTX_TOK // SWA_BLOCK
    cur = lambda b, i: (off + b * nb + i, 0)
    prev = lambda b, i: (off + b * nb + jnp.maximum(i - 1, 0), 0)
    nxt = lambda b, i: (off + b * nb + jnp.minimum(i + 1, nb - 1), 0)
    kv = lambda f: pl.BlockSpec((SWA_BLOCK, 128), f)
    cache = pl.BlockSpec((1, cache_k.shape[1], 128), lambda b, i: (b, 0, 0))
    return pl.pallas_call(
        _swa_lat_kernel,
        out_shape=jax.ShapeDtypeStruct((N_LAT_TOK, SWA_HEADS * SWA_HD), F32),
        grid=(N_LAT_SEQ, nb),
        in_specs=[pl.BlockSpec(memory_space=pltpu.SMEM),
                  pl.BlockSpec((SWA_BLOCK, 512), cur), cache, cache,
                  kv(prev), kv(nxt), kv(cur), kv(prev), kv(nxt), kv(cur)],
        out_specs=pl.BlockSpec((SWA_BLOCK, 512), lambda b, i: (b * nb + i, 0)),
        compiler_params=_params(),
        name="swa_lat",
    )(sink, sq, cache_k, cache_v, sk, sk, sk, sv, sv, sv)


def _l0_out_kernel(ogc_ref, ogl_ref, osc_ref, osl_ref, w_ref, xc_ref, xl_ref, mod_ref, n2_ref,
                   x1_ref, h2_ref):
    m = mod_ref[0]
    k_half = ogc_ref.shape[1]
    is_ctx = pl.program_id(0) < CTX_TILES

    def rows_of(first_row, n_rows):
        rows = slice(first_row, first_row + n_rows)
        pick = lambda c_ref, l_ref: jnp.where(is_ctx, c_ref[rows, :], l_ref[rows, :])
        y = (_dot(pick(ogc_ref, ogl_ref).astype(BF16), w_ref[:k_half])
             + _dot(pick(osc_ref, osl_ref).astype(BF16), w_ref[k_half:]))
        yield
        x1 = pick(xc_ref, xl_ref) + m[2:3] * y
        x1_ref[rows, :] = x1
        h2_ref[rows, :] = (_rms(x1, n2_ref[...]) * (1.0 + m[4:5]) + m[3:4]).astype(BF16)

    n_rows = x1_ref.shape[0] // 2
    _interleave([rows_of(0, n_rows), rows_of(n_rows, n_rows)])


def _l0_out(o_gla_c, o_gla_l, o_swa_c, o_swa_l, w, x_ctx, x_lat, mod, norm2):
    row = lambda n: pl.BlockSpec((TM, n), lambda i: (i, 0))
    full = lambda a: pl.BlockSpec(a.shape, lambda i: (0,) * a.ndim)
    return pl.pallas_call(
        _l0_out_kernel,
        out_shape=[jax.ShapeDtypeStruct((N_TOK, D_MODEL), F32),
                   jax.ShapeDtypeStruct((N_TOK, D_MODEL), BF16)],
        grid=(N_TILES,),
        in_specs=[_ctx_rows(512), _lat_rows(512), _ctx_rows(512), _lat_rows(512), full(w),
                  _ctx_rows(D_MODEL), _lat_rows(D_MODEL),
                  pl.BlockSpec((1, N_MOD, D_MODEL), lambda i: (_mod_index(i), 0, 0)), full(norm2)],
        out_specs=[row(D_MODEL), row(D_MODEL)],
        compiler_params=_params(),
        name="l0_out_proj",
    )(o_gla_c, o_gla_l, o_swa_c, o_swa_l, w, x_ctx, x_lat, mod, norm2)


def _ffn_kernel(h_ref, x_ref, w1_ref, w3_ref, w2_ref, mod0_ref, mod1_ref, n1_ref, x2_ref, h3_ref):
    h = h_ref[...]
    acc = jnp.zeros((h.shape[0], D_MODEL), F32)
    for j in range(w1_ref.shape[1] // FF_CHUNK):
        cols = slice(j * FF_CHUNK, (j + 1) * FF_CHUNK)
        g = (_silu(_dot(h, w1_ref[:, cols])) * _dot(h, w3_ref[:, cols])).astype(BF16)
        acc = acc + _dot(g, w2_ref[cols, :])
    m0 = mod0_ref[0]
    m1 = mod1_ref[0]
    x2 = x_ref[...] + m0[5:6] * acc
    x2_ref[...] = x2
    h3_ref[...] = (_rms(x2, n1_ref[...]) * (1.0 + m1[1:2]) + m1[0:1]).astype(BF16)


def _ffn(h2, x1, w1, w3, w2, mod0, mod1, norm1_next):
    row = lambda n: pl.BlockSpec((TM, n), lambda i: (i, 0))
    full = lambda a: pl.BlockSpec(a.shape, lambda i: (0,) * a.ndim)
    mod = pl.BlockSpec((1, N_MOD, D_MODEL), lambda i: (_mod_index(i), 0, 0))
    return pl.pallas_call(
        _ffn_kernel,
        out_shape=[jax.ShapeDtypeStruct((N_TOK, D_MODEL), F32),
                   jax.ShapeDtypeStruct((N_TOK, D_MODEL), BF16)],
        grid=(N_TILES,),
        in_specs=[row(D_MODEL), row(D_MODEL), full(w1), full(w3), full(w2), mod, mod,
                  full(norm1_next)],
        out_specs=[row(D_MODEL), row(D_MODEL)],
        compiler_params=_params(),
        name="dense_ffn",
    )(h2, x1, w1, w3, w2, mod0, mod1, norm1_next)


L1_Z = DN_QKV
L1_AB = DN_QKV + DN_HEADS * DN_DV
L1_END = L1_AB + LANES
N_GATE = 4 * DN_HEADS


def _dn_gates(ab, neg_a, dt_bias, is_decay):
    return jnp.where(is_decay, neg_a * _softplus(ab + dt_bias), _sigmoid(ab))


def _l1_in_kernel(h_ref, w_ref, wt_ref, alog_ref, dtb_ref, alogc_ref, dtbc_ref,
                  qkv_ref, z_ref, g_ref, gt_ref):
    h = h_ref[...]
    qkv_ref[...] = _dot(h, w_ref[:, :L1_Z])
    z_ref[...] = _dot(h, w_ref[:, L1_Z:L1_AB])
    ab = _dot(h, w_ref[:, L1_AB:L1_END])
    lane = lax.broadcasted_iota(jnp.int32, ab.shape, 1)
    g_ref[...] = _dn_gates(ab, -jnp.exp(alog_ref[...]), dtb_ref[...], lane < 2 * DN_HEADS)
    abt = _dot_nt(wt_ref[...], h)
    row = lax.broadcasted_iota(jnp.int32, abt.shape, 0)
    gt_ref[...] = _dn_gates(abt, -jnp.exp(alogc_ref[...]), dtbc_ref[...], row < 2 * DN_HEADS)


def _l1_in(h3, w, wt, alog, dtb, alogc, dtbc):
    row = lambda n: pl.BlockSpec((TM, n), lambda i: (i, 0))
    full = lambda a: pl.BlockSpec(a.shape, lambda i: (0,) * a.ndim)
    return pl.pallas_call(
        _l1_in_kernel,
        out_shape=[jax.ShapeDtypeStruct((N_TOK, DN_QKV), F32),
                   jax.ShapeDtypeStruct((N_TOK, DN_HEADS * DN_DV), F32),
                   jax.ShapeDtypeStruct((N_TOK, LANES), F32),
                   jax.ShapeDtypeStruct((N_GATE, N_TOK), F32)],
        grid=(N_TILES,),
        in_specs=[row(D_MODEL), full(w), full(wt), full(alog), full(dtb), full(alogc), full(dtbc)],
        out_specs=[row(DN_QKV), row(DN_HEADS * DN_DV), row(LANES),
                   pl.BlockSpec((N_GATE, TM), lambda i: (0, i))],
        compiler_params=_params(),
        name="l1_in_proj",
    )(h3, w, wt, alog, dtb, alogc, dtbc)


HALO = SUBLANES
SOLVE_BASE = 16


def _unit_tri_inverse(a, blk16, blk32):
    x = jnp.where(blk16, -a, 0.0)
    xb = x.astype(BF16)
    x2 = _dot(xb, xb)
    yield
    x2b = x2.astype(BF16)
    z = _dot(jnp.concatenate([x2b, xb], axis=0), x2b)
    yield
    x4 = z[:CHUNK]
    s = (x + x2) + z[CHUNK:]
    x4b = x4.astype(BF16)
    z = _dot(jnp.concatenate([x4b, s.astype(BF16)], axis=0), x4b)
    yield
    s = (s + x4) + z[CHUNK:]
    x8 = z[:CHUNK]
    z = _dot(s.astype(BF16), x8.astype(BF16))
    yield
    s = (s + x8) + z
    for off in (jnp.where(blk32 & ~blk16, a, 0.0), jnp.where(blk32, 0.0, a)):
        sb = s.astype(BF16)
        z = _dot(sb, off.astype(BF16))
        yield
        y = off + z
        z = _dot(y.astype(BF16), sb)
        yield
        s = s - (y + z)
    return s


def _dn_kernel(q_ref, k_ref, v_ref, cwq_ref, cwk_ref, cwv_ref, g_ref, gt_ref, s0_ref,
               o_ref, st_ref,
               pad_ref, wq_ref, u_ref, ke_ref, dl_ref, *, seq_len, heads, unroll, scan_steps):
    n = seq_len // CHUNK
    h0 = pl.program_id(1) * heads
    phase = pl.program_id(1) % scan_steps
    slot0 = phase * heads
    lower, upper = _tri_masks(CHUNK)
    strict_lower = lower & ~upper
    strict_upper = upper & ~lower
    lower_bf = lower.astype(BF16)
    upper_bf = upper.astype(BF16)
    ri = lax.broadcasted_iota(jnp.int32, (CHUNK, CHUNK), 0)
    ci = lax.broadcasted_iota(jnp.int32, (CHUNK, CHUNK), 1)
    blk16 = (ri // SOLVE_BASE) == (ci // SOLVE_BASE)
    blk32 = (ri // (2 * SOLVE_BASE)) == (ci // (2 * SOLVE_BASE))

    zero = jnp.zeros((HALO, DN_DK), F32)
    for t, src in enumerate((q_ref, k_ref, v_ref)):
        for hh in range(heads):
            pad_ref[t, hh, 0:HALO] = zero
            pad_ref[t, hh, HALO + seq_len:2 * HALO + seq_len] = zero
            pad_ref[t, hh, HALO:HALO + seq_len] = src[:, hh * DN_DK:(hh + 1) * DN_DK]

    @pl.when(phase == 0)
    def _():
        o_ref[...] = jnp.zeros_like(o_ref)
        st_ref[...] = s0_ref[...]

    def conv(t, cw_ref, r0, hh):
        acc = None
        for j in range(DN_CONV):
            term = (pad_ref[t, hh, pl.ds(r0 + (HALO - DN_CONV // 2 + j), CHUNK)]
                    * cw_ref[j:j + 1, hh * DN_DK:(hh + 1) * DN_DK])
            acc = term if acc is None else acc + term
        return _silu(acc)

    def l2n(x):
        return x * lax.rsqrt(jnp.sum(x * x, axis=-1, keepdims=True) + EPS)

    lane = lax.broadcasted_iota(jnp.int32, (CHUNK, LANES), 1)

    def gate_col(gblk, col):
        return jnp.sum(jnp.where(lane == col, gblk, 0.0), axis=-1, keepdims=True)

    directions = ((lower, strict_lower, CHUNK - 1), (upper, strict_upper, 0))

    gate_row = lax.broadcasted_iota(jnp.int32, (N_GATE, CHUNK), 0)

    def prep_chain(c, hh, d, q, k, v, gblk, gcum, grow):
        mask, strict, tr = directions[d]
        beta = gate_col(gblk, h0 + hh + (2 + d) * DN_HEADS)
        gc = jnp.broadcast_to(gate_col(gcum[d], h0 + hh + d * DN_HEADS), (CHUNK, DN_DK))
        gr = jnp.sum(jnp.where(gate_row == h0 + hh + d * DN_HEADS, grow[d], 0.0), axis=0, keepdims=True)
        k_beta = k * beta
        kq = _dot_nt(jnp.concatenate([k_beta, q], axis=0).astype(BF16), k.astype(BF16))
        yield
        decay = jnp.exp(jnp.where(mask, gc[:, :CHUNK] - gr, -jnp.inf))
        a = jnp.where(strict, kq[:CHUNK] * decay, 0.0)
        qk = kq[CHUNK:] * decay
        eg = jnp.exp(gc)
        rhs = jnp.concatenate([v * beta, k_beta * eg], axis=-1)
        s = yield from _unit_tri_inverse(a, blk16, blk32)
        z = _dot(s.astype(BF16), rhs.astype(BF16))
        yield
        r = rhs + z
        tot = gc[tr:tr + 1]
        idx = (2 * (slot0 + hh) + d) * n + c
        u_ref[idx] = r[:, :DN_DV].astype(BF16)
        qk_rows = jnp.concatenate([qk, jnp.zeros((CHUNK, DN_DK - CHUNK), F32)], axis=-1)
        wq_ref[idx] = jnp.concatenate([r[:, DN_DV:], q * eg, qk_rows], axis=0).astype(BF16)
        ke_ref[idx] = (k * jnp.exp(tot - gc)).astype(BF16)
        dl_ref[idx] = jnp.broadcast_to(jnp.exp(tot), (SUBLANES, DN_DV))

    def prep(cg, carry):
        chains = []
        for uu in range(unroll):
            c = cg * unroll + uu
            r0 = pl.multiple_of(c * CHUNK, CHUNK)
            gblk = g_ref[pl.ds(r0, CHUNK)]
            gtblk = gt_ref[c]
            gcum = (_mask_dot(lower_bf, gblk), _mask_dot(upper_bf, gblk))
            grow = (_dot_mask(gtblk, upper_bf), _dot_mask(gtblk, lower_bf))
            for hh in range(heads):
                q = l2n(conv(0, cwq_ref, r0, hh)) * (DN_DK ** -0.5)
                k = l2n(conv(1, cwk_ref, r0, hh))
                v = conv(2, cwv_ref, r0, hh)
                for d in range(2):
                    chains.append(prep_chain(c, hh, d, q, k, v, gblk, gcum, grow))
        _interleave(chains)
        return carry

    lax.fori_loop(0, n // unroll, prep, 0)

    def scan_chain(c, hh, d):
        cc = c if d == 0 else n - 1 - c
        idx = (2 * hh + d) * n + cc
        s = st_ref[0, d, hh]
        rows = pl.ds(pl.multiple_of(cc * CHUNK, CHUNK), CHUNK)
        cols = slice(hh * DN_DV, (hh + 1) * DN_DV)
        z = _dot(wq_ref[idx, 0:2 * CHUNK], s.astype(BF16))
        yield
        v_new = (u_ref[idx].astype(F32) - z[:CHUNK]).astype(BF16)
        zo = _dot(wq_ref[idx, 2 * CHUNK:3 * CHUNK, 0:CHUNK], v_new)
        zs = _dot_tn(ke_ref[idx], v_new)
        yield
        o_ref[rows, cols] += z[CHUNK:] + zo
        st_ref[0, d, hh] = s * dl_ref[idx][0:1] + zs

    def scan(c, carry):
        _interleave([scan_chain(c, hh, d) for hh in range(heads * scan_steps) for d in range(2)])
        return carry

    @pl.when(phase == scan_steps - 1)
    def _():
        lax.fori_loop(0, n, scan, 0)


def _deltanet(qkv, conv_w, gates, gates_t, s0, *, seq_len, n_seq, row_off, heads, unroll, scan_steps):
    off = row_off // seq_len
    n = seq_len // CHUNK
    groups = DN_HEADS // heads
    width = heads * DN_DK
    scan_heads = heads * scan_steps
    col = lambda t: pl.BlockSpec((seq_len, width), lambda s, p: (s + off, t * groups + p))
    cw = lambda t: pl.BlockSpec((DN_CONV, width), lambda s, p: (0, t * groups + p))
    st_spec = pl.BlockSpec((1, 2, scan_heads, DN_DK, DN_DV), lambda s, p: (s, 0, p // scan_steps, 0, 0))
    slots = 2 * scan_heads * n
    return pl.pallas_call(
        functools.partial(_dn_kernel, seq_len=seq_len, heads=heads, unroll=unroll, scan_steps=scan_steps),
        out_shape=[jax.ShapeDtypeStruct((n_seq * seq_len, DN_HEADS * DN_DV), F32),
                   jax.ShapeDtypeStruct((n_seq, 2, DN_HEADS, DN_DK, DN_DV), F32)],
        grid=(n_seq, groups),
        in_specs=[col(0), col(1), col(2), cw(0), cw(1), cw(2),
                  pl.BlockSpec((seq_len, LANES), lambda s, p: (s + off, 0)),
                  pl.BlockSpec((n, N_GATE, CHUNK), lambda s, p: (s + off, 0, 0)),
                  st_spec],
        out_specs=[pl.BlockSpec((seq_len, scan_heads * DN_DV), lambda s, p: (s, p // scan_steps)), st_spec],
        scratch_shapes=[pltpu.VMEM((3, heads, seq_len + 2 * HALO, DN_DK), F32),
                        pltpu.VMEM((slots, 3 * CHUNK, DN_DK), BF16),
                        pltpu.VMEM((slots, CHUNK, DN_DV), BF16),
                        pltpu.VMEM((slots, CHUNK, DN_DK), BF16),
                        pltpu.VMEM((slots, SUBLANES, DN_DV), F32)],
        compiler_params=_params(),
        name=f"deltanet_{seq_len}",
    )(qkv, qkv, qkv, conv_w, conv_w, conv_w, gates, gates_t, s0)


def _l1_out_kernel(oc_ref, ol_ref, z_ref, on_ref, w_ref, x_ref, mod_ref, n2_ref, rh_ref, rl_ref,
                   x3_ref, h4_ref, route_ref):
    m = mod_ref[0]
    on = on_ref[...]
    is_ctx = pl.program_id(0) < CTX_TILES

    def half(first_row, n_rows):
        rows = slice(first_row, first_row + n_rows)
        parts = []
        for hd in range(DN_HEADS):
            cs = slice(hd * DN_DV, (hd + 1) * DN_DV)
            o = jnp.where(is_ctx, oc_ref[rows, cs], ol_ref[rows, cs])
            parts.append((_rms(o, on) * _silu(z_ref[rows, cs])).astype(BF16))
        y = _dot(jnp.concatenate(parts, axis=-1), w_ref[...])
        yield
        x3 = x_ref[rows, :] + m[2:3] * y
        x3_ref[rows, :] = x3
        h4 = _rms(x3, n2_ref[...]) * (1.0 + m[4:5]) + m[3:4]
        for s in range(ROW_TILES):
            h4_ref[pl.ds(first_row * ROW_TILES + s, n_rows, stride=ROW_TILES), :] = h4[:, s * LANES:(s + 1) * LANES]
        h4_hi = h4.astype(BF16)
        h4_lo = (h4 - h4_hi.astype(F32)).astype(BF16)
        logits = (_dot(h4_hi, rh_ref[...]) + _dot(h4_hi, rl_ref[...])) + _dot(h4_lo, rh_ref[...])
        yield
        lane = lax.broadcasted_iota(jnp.int32, logits.shape, 1).astype(F32)
        logits = jnp.where(lane < N_EXPERTS, logits, -jnp.inf)
        m1 = jnp.max(logits, axis=-1, keepdims=True)
        i1 = jnp.min(jnp.where(logits == m1, lane, float(LANES)), axis=-1, keepdims=True)
        rest = jnp.where(lane == i1, -jnp.inf, logits)
        m2 = jnp.max(rest, axis=-1, keepdims=True)
        i2 = jnp.min(jnp.where(rest == m2, lane, float(LANES)), axis=-1, keepdims=True)
        e2 = jnp.exp(m2 - m1)
        g1 = 1.0 / (1.0 + e2)
        g2 = e2 / (1.0 + e2)
        route_ref[rows, :] = jnp.where(lane == 0, i1, jnp.where(lane == 1, i2, jnp.where(
            lane == 2, g1, jnp.where(lane == 3, g2, 0.0))))

    n_rows = x_ref.shape[0] // 2
    _interleave([half(0, n_rows), half(n_rows, n_rows)])


def _l1_out(o_dn_c, o_dn_l, z, onorm, w, x2, mod, norm2, r_hi, r_lo):
    row = lambda n: pl.BlockSpec((TM, n), lambda i: (i, 0))
    full = lambda a: pl.BlockSpec(a.shape, lambda i: (0,) * a.ndim)
    return pl.pallas_call(
        _l1_out_kernel,
        out_shape=[jax.ShapeDtypeStruct((N_TOK, D_MODEL), F32),
                   jax.ShapeDtypeStruct((N_TOK * ROW_TILES, LANES), F32),
                   jax.ShapeDtypeStruct((N_TOK, LANES), F32)],
        grid=(N_TILES,),
        in_specs=[_ctx_rows(D_MODEL), _lat_rows(D_MODEL), row(D_MODEL), full(onorm), full(w),
                  row(D_MODEL), pl.BlockSpec((1, N_MOD, D_MODEL), lambda i: (_mod_index(i), 0, 0)),
                  full(norm2), full(r_hi), full(r_lo)],
        out_specs=[row(D_MODEL), pl.BlockSpec((TM * ROW_TILES, LANES), lambda i: (i, 0)), row(LANES)],
        compiler_params=_params(),
        name="l1_out_proj_router",
    )(o_dn_c, o_dn_l, z, onorm, w, x2, mod, norm2, r_hi, r_lo)


ROW_DMA_UNROLL = 8


def _moe_kernel(te_ref, nt_ref, nv_ref, src_ref, dst_ref, h_hbm, w1_ref, w3_ref, w2_ref, y_hbm,
                xs_ref, xb_ref, acc_ref, yb_ref, gsem, ssem):
    i = pl.program_id(0)
    j = pl.program_id(1)
    n_used = nt_ref[0]
    slot = i % 2

    def for_valid_rows(tile, fn):
        nv = nv_ref[tile]
        groups = nv // ROW_DMA_UNROLL

        def group(g, carry):
            for u in range(ROW_DMA_UNROLL):
                fn(g * ROW_DMA_UNROLL + u)
            return carry

        def single(r, carry):
            fn(r)
            return carry

        lax.fori_loop(0, groups, group, 0)
        lax.fori_loop(groups * ROW_DMA_UNROLL, nv, single, 0)

    def row_tile(n):
        return pl.ds(pl.multiple_of(n * ROW_TILES, ROW_TILES), ROW_TILES)

    def gather(tile, sl, start):
        def row(r):
            cp = pltpu.make_async_copy(h_hbm.at[row_tile(src_ref[tile * MOE_TM + r])],
                                       xs_ref.at[sl, row_tile(r)], gsem.at[sl])
            cp.start() if start else cp.wait()
        for_valid_rows(tile, row)

    def scatter(tile, sl, start):
        def row(r):
            cp = pltpu.make_async_copy(yb_ref.at[sl, row_tile(r)],
                                       y_hbm.at[row_tile(dst_ref[tile * MOE_TM + r])], ssem.at[sl])
            cp.start() if start else cp.wait()
        for_valid_rows(tile, row)

    @pl.when((i < n_used) & (j == 0))
    def _():
        @pl.when(i == 0)
        def _():
            xs_ref[...] = jnp.zeros_like(xs_ref)
            gather(i, slot, True)
        gather(i, slot, False)

        @pl.when(i + 1 < n_used)
        def _():
            gather(i + 1, 1 - slot, True)
        xb_ref[...] = _load_row_tiles(xs_ref, MOE_TM, (slot,)).astype(BF16)
        acc_ref[...] = jnp.zeros_like(acc_ref)

    @pl.when(i < n_used)
    def _():
        x = xb_ref[...]
        g = (_silu(_dot(x, w1_ref[0])) * _dot(x, w3_ref[0])).astype(BF16)
        acc_ref[...] += _dot(g, w2_ref[0])

    @pl.when((i < n_used) & (j == pl.num_programs(1) - 1))
    def _():
        @pl.when(i >= 2)
        def _():
            scatter(i - 2, slot, False)
        _store_row_tiles(yb_ref, acc_ref[...], (slot,))
        scatter(i, slot, True)

    @pl.when((i == pl.num_programs(0) - 1) & (j == pl.num_programs(1) - 1))
    def _():
        @pl.when(n_used >= 2)
        def _():
            scatter(n_used - 2, n_used % 2, False)
        scatter(n_used - 1, (n_used - 1) % 2, False)


def _moe(tile_expert, n_used, n_valid, src_rows, dst_rows, h4, w1, w3, w2):
    def wmap(i, j, te, nt, nv, src, dst):
        return (te[i], 0, jnp.where(i < nt[0], j, MOE_FF_STEPS - 1))

    def w2map(i, j, te, nt, nv, src, dst):
        return (te[i], jnp.where(i < nt[0], j, MOE_FF_STEPS - 1), 0)

    grid_spec = pltpu.PrefetchScalarGridSpec(
        num_scalar_prefetch=5,
        grid=(MOE_TILES, MOE_FF_STEPS),
        in_specs=[pl.BlockSpec(memory_space=pl.ANY),
                  pl.BlockSpec((1, D_MODEL, MOE_FF), wmap),
                  pl.BlockSpec((1, D_MODEL, MOE_FF), wmap),
                  pl.BlockSpec((1, MOE_FF, D_MODEL), w2map)],
        out_specs=pl.BlockSpec(memory_space=pl.ANY),
        scratch_shapes=[pltpu.VMEM((2, MOE_TM * ROW_TILES, LANES), F32),
                        pltpu.VMEM((MOE_TM, D_MODEL), BF16),
                        pltpu.VMEM((MOE_TM, D_MODEL), F32),
                        pltpu.VMEM((2, MOE_TM * ROW_TILES, LANES), F32),
                        pltpu.SemaphoreType.DMA((2,)),
                        pltpu.SemaphoreType.DMA((2,))])
    return pl.pallas_call(
        _moe_kernel,
        out_shape=jax.ShapeDtypeStruct((2 * N_TOK * ROW_TILES, LANES), F32),
        grid_spec=grid_spec,
        compiler_params=_params(has_side_effects=True),
        name="routed_experts",
    )(tile_expert, n_used, n_valid, src_rows, dst_rows, h4, w1, w3, w2)


def _combine_kernel(y0_ref, y1_ref, route_ref, x_ref, mod_ref, fn_ref, o_ref):
    m = mod_ref[0]
    route = route_ref[...]
    y = (route[:, 2:3] * _load_row_tiles(y0_ref, TM)) + (route[:, 3:4] * _load_row_tiles(y1_ref, TM))
    o_ref[...] = _rms(x_ref[...] + m[5:6] * y, fn_ref[...])


def _combine(y, route, x3, mod, final_norm, *, first_tile, n_tiles):
    row = lambda n: pl.BlockSpec((TM, n), lambda i: (i + first_tile, 0))
    slot = lambda k: pl.BlockSpec((TM * ROW_TILES, LANES), lambda i: (i + first_tile + k * N_TILES, 0))
    return pl.pallas_call(
        _combine_kernel,
        out_shape=jax.ShapeDtypeStruct((n_tiles * TM, D_MODEL), F32),
        grid=(n_tiles,),
        in_specs=[slot(0), slot(1), row(LANES), row(D_MODEL),
                  pl.BlockSpec((1, N_MOD, D_MODEL), lambda i: (_mod_index(i + first_tile), 0, 0)),
                  pl.BlockSpec((1, D_MODEL), lambda i: (0, 0))],
        out_specs=pl.BlockSpec((TM, D_MODEL), lambda i: (i, 0)),
        compiler_params=_params(),
        name="combine_final_norm",
    )(y, y, route, x3, mod, final_norm)


def _rope_tables():
    rows = LAT_LEN // GRID_W
    row = jnp.repeat(jnp.arange(rows), GRID_W).astype(F32)
    col = (jnp.arange(rows * GRID_W) % GRID_W).astype(F32)
    half = SWA_HD // 2
    inv_freq = ROPE_BASE ** (-jnp.arange(0, half, 2, dtype=F32) / half)
    ang_r = row[:, None] * inv_freq
    ang_c = col[:, None] * inv_freq
    cos = jnp.concatenate([jnp.cos(ang_r)] * 2 + [jnp.cos(ang_c)] * 2, axis=-1)
    sin = jnp.concatenate([-jnp.sin(ang_r), jnp.sin(ang_r), -jnp.sin(ang_c), jnp.sin(ang_c)], axis=-1)
    cos = jnp.tile(cos, (1, SWA_HEADS))
    sin = jnp.tile(sin, (1, SWA_HEADS))
    cos = jnp.concatenate([cos, jnp.ones((TM, cos.shape[1]), F32)], axis=0)
    sin = jnp.concatenate([sin, jnp.zeros((TM, sin.shape[1]), F32)], axis=0)
    return cos, sin


def _dispatch(route):
    e = jnp.concatenate([route[:, 0], route[:, 1]]).astype(jnp.int32)
    onehot = (e[:, None] == jnp.arange(N_EXPERTS)[None, :]).astype(jnp.int32)
    counts = jnp.sum(onehot, axis=0)
    rank = jnp.sum((jnp.cumsum(onehot, axis=0) - onehot) * onehot, axis=1)
    padded = ((counts + MOE_TM - 1) // MOE_TM) * MOE_TM
    ends = jnp.cumsum(padded)
    starts = ends - padded
    pos = starts[e] + rank
    dst = jnp.zeros((MOE_ROWS,), jnp.int32).at[pos].set(jnp.arange(2 * N_TOK, dtype=jnp.int32))
    src = dst % N_TOK
    n_used = (ends[-1] // MOE_TM).astype(jnp.int32)
    tile_start = jnp.arange(MOE_TILES, dtype=jnp.int32) * MOE_TM
    tile_expert = jnp.minimum(jnp.sum(tile_start[:, None] >= ends[None, :], axis=1), N_EXPERTS - 1)
    n_valid = jnp.clip((starts + counts)[tile_expert] - tile_start, 0, MOE_TM).astype(jnp.int32)
    last = jnp.take(tile_expert, jnp.maximum(n_used - 1, 0))
    tile_expert = jnp.where(jnp.arange(MOE_TILES) < n_used, tile_expert, last).astype(jnp.int32)
    return tile_expert, n_used.reshape(1), n_valid, src, dst


def kernel(x_prompt, x_sample, c, state_gla_l0, cache_k_l0, cache_v_l0, state_dn_l1, c_ctx, final_norm, ada_w_l0, ada_b_l0, norm1_l0, norm2_l0, in_w_l0, gla_w2_l0, gla_b_l0, gla_onorm_l0, sink_l0, out_w_l0, ffn_w1_l0, ffn_w3_l0, ffn_w2_l0, ada_w_l1, ada_b_l1, norm1_l1, norm2_l1, in_w_l1, conv_w_l1, a_log_l1, dt_bias_l1, dn_onorm_l1, out_w_l1, router_l1, moe_w1_l1, moe_w3_l1, moe_w2_l1):
    D = D_MODEL
    x_ctx = x_prompt.reshape(N_CTX_TOK, D)
    x_lat = x_sample.reshape(N_LAT_TOK, D)
    row2 = lambda v: v.reshape(1, -1)

    cond = jnp.concatenate([c_ctx[None], c, jnp.zeros((2 * SUBLANES - 1 - N_LAT_SEQ, D), F32)], axis=0)
    mod0 = _modulation(cond, ada_w_l0, ada_b_l0).reshape(-1, N_MOD, D)
    mod1 = _modulation(cond, ada_w_l1, ada_b_l1).reshape(-1, N_MOD, D)

    gq_w, gk_w, gv_w, gr_w, glrf_w, glrb_w, sq_w, sk_w, sv_w = jnp.split(
        in_w_l0, [256, 512, 1024, 1536, 1552, 1568, 2080, 2208], axis=1)
    w0 = jnp.concatenate([gq_w, gk_w, gv_w, gr_w, sq_w, sk_w, sv_w, glrf_w, glrb_w,
                          jnp.zeros((D, L0_END - L0_GLR - 2 * GLA_RANK), F32)], axis=1).astype(BF16)
    w2f = jnp.zeros((LANES, GLA_HEADS * GLA_DK), F32).at[:GLA_RANK].set(gla_w2_l0[0]).astype(BF16)
    w2b = jnp.zeros((LANES, GLA_HEADS * GLA_DK), F32).at[GLA_RANK:2 * GLA_RANK].set(gla_w2_l0[1]).astype(BF16)
    cos, sin = _rope_tables()
    gq, gk, gv, gr, laf, lab, sq, sk, sv = _l0_in(x_ctx, x_lat, mod0, row2(norm1_l0), w0, w2f, w2b,
                                                  gla_b_l0, cos, sin)

    onorm0 = row2(gla_onorm_l0)
    zero_state = jnp.zeros((N_CTX_SEQ, 2, GLA_HEADS, GLA_DV, GLA_DK), F32)
    o_gla_c, st_gla = _gla(gq, gk, gv, gr, laf, lab, zero_state, onorm0,
                           seq_len=CTX_LEN, n_seq=N_CTX_SEQ, row_off=0)
    o_gla_l, _ = _gla(gq, gk, gv, gr, laf, lab, jnp.swapaxes(state_gla_l0, -1, -2), onorm0,
                      seq_len=LAT_LEN, n_seq=N_LAT_SEQ, row_off=N_CTX_TOK)
    o_swa_c = _swa_ctx(sink_l0, sq, sk, sv)
    o_swa_l = _swa_lat(sink_l0, sq, sk, sv,
                       cache_k_l0.reshape(N_LAT_SEQ, -1, SWA_KV_HEADS * SWA_HD),
                       cache_v_l0.reshape(N_LAT_SEQ, -1, SWA_KV_HEADS * SWA_HD))
    x1, h2 = _l0_out(o_gla_c, o_gla_l, o_swa_c, o_swa_l, out_w_l0.astype(BF16), x_ctx, x_lat, mod0,
                     row2(norm2_l0))

    x2, h3 = _ffn(h2, x1, ffn_w1_l0.astype(BF16), ffn_w3_l0.astype(BF16), ffn_w2_l0.astype(BF16),
                  mod0, mod1, row2(norm1_l1))

    w_ab = in_w_l1[:, L1_AB:]
    wl1 = jnp.concatenate([in_w_l1, jnp.zeros((D, L1_END - in_w_l1.shape[1]), F32)], axis=1).astype(BF16)
    zeros_g = jnp.zeros((2 * DN_HEADS,), F32)
    alog = jnp.concatenate([a_log_l1.reshape(-1), zeros_g])
    dtb = jnp.concatenate([dt_bias_l1.reshape(-1), zeros_g])
    pad_lane = lambda v: jnp.concatenate([v, jnp.zeros((LANES - N_GATE,), F32)]).reshape(1, LANES)
    qkv, z, gates, gates_t = _l1_in(h3, wl1, w_ab.T.astype(BF16), pad_lane(alog), pad_lane(dtb),
                                    alog.reshape(N_GATE, 1), dtb.reshape(N_GATE, 1))
    gates_t = gates_t.reshape(N_GATE, N_TOK // CHUNK, CHUNK).transpose(1, 0, 2)
    zero_dn = jnp.zeros((N_CTX_SEQ, 2, DN_HEADS, DN_DK, DN_DV), F32)
    o_dn_c, st_dn = _deltanet(qkv, conv_w_l1, gates, gates_t, zero_dn,
                              seq_len=CTX_LEN, n_seq=N_CTX_SEQ, row_off=0, heads=4, unroll=4, scan_steps=1)
    o_dn_l, _ = _deltanet(qkv, conv_w_l1, gates, gates_t, state_dn_l1,
                          seq_len=LAT_LEN, n_seq=N_LAT_SEQ, row_off=N_CTX_TOK, heads=2, unroll=4,
                          scan_steps=2)
    router = jnp.concatenate([router_l1, jnp.zeros((D, LANES - N_EXPERTS), F32)], axis=1)
    r_hi = router.astype(BF16)
    r_lo = (router - r_hi.astype(F32)).astype(BF16)
    x3, h4, route = _l1_out(o_dn_c, o_dn_l, z, row2(dn_onorm_l1), out_w_l1.astype(BF16), x2, mod1,
                            row2(norm2_l1), r_hi, r_lo)

    tile_expert, n_used, n_valid, src, dst = _dispatch(route)
    y = _moe(tile_expert, n_used, n_valid, src, dst, h4,
             moe_w1_l1.astype(BF16), moe_w3_l1.astype(BF16), moe_w2_l1.astype(BF16))
    fn = row2(final_norm)
    y_prompt = _combine(y, route, x3, mod1, fn, first_tile=0, n_tiles=CTX_TILES)
    y_sample = _combine(y, route, x3, mod1, fn, first_tile=CTX_TILES, n_tiles=N_TILES - CTX_TILES)
    y_prompt = y_prompt.reshape(N_CTX_SEQ, CTX_LEN, D)
    y_sample = y_sample.reshape(N_LAT_SEQ, LAT_LEN, D)
    new_state_gla = jnp.swapaxes(st_gla, -1, -2)
    new_k = sk[:N_CTX_TOK].reshape(N_CTX_SEQ, CTX_LEN, SWA_KV_HEADS, SWA_HD)
    new_v = sv[:N_CTX_TOK].reshape(N_CTX_SEQ, CTX_LEN, SWA_KV_HEADS, SWA_HD)
    return (y_prompt, y_sample, new_state_gla, new_k, new_v, st_dn)
```
